```python
import math
import jax, jax.numpy as jnp
from jax import lax
import numpy as np

D_MODEL = 1024
BATCH = 2
SEQ = 8192
DEPTH = 1
DEC_BATCH = 8
DEC_SEQ = 64
PAST_LEN = 2048

CHUNK = 64
N_SUBLAYERS = 3
FFN_RES = 0.5
FF_DIM = 2816
RMS_EPS = 1e-6
S5_WIDTH = D_MODEL // 2
S5_GROUP = 16
S5_GROUPS = S5_WIDTH // S5_GROUP
S5_STATE = 64
HEAD_DIM = 64
N_HEADS = (D_MODEL // 2) // HEAD_DIM
N_KV_HEADS = 2
GQA_REP = N_HEADS // N_KV_HEADS
WINDOW = 128
WIN_CHUNKS = WINDOW // CHUNK
WIN_CACHE = min(WINDOW, PAST_LEN)
ROPE_THETA = 500000.0
ROPE_DIM = HEAD_DIM // 4
ATTN_SCALE = HEAD_DIM ** -0.5
NEG_INF = -1e30
Q_WIDTH = N_HEADS * HEAD_DIM
KV_WIDTH = N_KV_HEADS * HEAD_DIM
IN_WIDTH = S5_WIDTH + Q_WIDTH + 2 * KV_WIDTH + 2 * D_MODEL
IN_SPLITS = (S5_WIDTH,
             S5_WIDTH + Q_WIDTH,
             S5_WIDTH + Q_WIDTH + KV_WIDTH,
             S5_WIDTH + Q_WIDTH + 2 * KV_WIDTH,
             S5_WIDTH + Q_WIDTH + 2 * KV_WIDTH + D_MODEL)

kernel_name = 'hybrid_s5_swa_macaron_stream_step'


def rmsnorm(x, g):
    xf = x.astype(jnp.float32)
    y = xf * lax.rsqrt(jnp.mean(xf * xf, axis=-1, keepdims=True) + RMS_EPS)
    return (y * g.astype(jnp.float32)).astype(x.dtype)


def modulate(x, g, shift, scale):
    return rmsnorm(x, g) * (1.0 + scale) + shift


def swiglu(h, w_in, w_out):
    a, b = jnp.split(h @ w_in, 2, axis=-1)
    return (jax.nn.silu(a) * b) @ w_out


def adaln_params(c, w_ada, b_ada):
    mod = jax.nn.silu(c) @ w_ada + b_ada
    return mod.reshape(c.shape[0], N_SUBLAYERS, 3, 1, D_MODEL)


def partial_rope(x, pos):
    half = ROPE_DIM // 2
    inv_freq = ROPE_THETA ** (-jnp.arange(half, dtype=jnp.float32) / half)
    ang = pos.astype(jnp.float32)[:, None] * inv_freq[None, :]
    cos = jnp.cos(ang)[None, :, None, :]
    sin = jnp.sin(ang)[None, :, None, :]
    xr = x[..., :ROPE_DIM].astype(jnp.float32)
    x1, x2 = xr[..., :half], xr[..., half:]
    rot = jnp.concatenate([x1 * cos - x2 * sin, x2 * cos + x1 * sin], axis=-1)
    return jnp.concatenate([rot.astype(x.dtype), x[..., ROPE_DIM:]], axis=-1)


def project_inputs(u, w_in, q_norm, k_norm, pos):
    b, l = u.shape[:2]
    s5_u, q, k, v, g_s5, g_attn = jnp.split(u @ w_in, IN_SPLITS, axis=-1)
    q = q.reshape(b, l, N_HEADS, HEAD_DIM)
    k = k.reshape(b, l, N_KV_HEADS, HEAD_DIM)
    v = v.reshape(b, l, N_KV_HEADS, HEAD_DIM)
    q = partial_rope(rmsnorm(q, q_norm), pos)
    k = partial_rope(rmsnorm(k, k_norm), pos)
    return s5_u, q, k, v, jax.nn.sigmoid(g_s5), jax.nn.sigmoid(g_attn)


def s5_discretise(log_dt, a_re, a_im, b_re, b_im):
    dt = jnp.exp(log_dt.astype(jnp.float32))[:, None]
    lr = jnp.minimum(a_re.astype(jnp.float32), -1e-4)
    li = a_im.astype(jnp.float32)
    mag = jnp.exp(lr * dt)
    ab_re = mag * jnp.cos(li * dt)
    ab_im = mag * jnp.sin(li * dt)
    nr, ni = ab_re - 1.0, ab_im
    den = lr * lr + li * li
    f_re = (nr * lr + ni * li) / den
    f_im = (ni * lr - nr * li) / den
    br, bi = b_re.astype(jnp.float32), b_im.astype(jnp.float32)
    bb_re = f_re[..., None] * br - f_im[..., None] * bi
    bb_im = f_re[..., None] * bi + f_im[..., None] * br
    return ab_re, ab_im, bb_re, bb_im


def s5_combine(e1, e2):
    a1r, a1i, b1r, b1i = e1
    a2r, a2i, b2r, b2i = e2
    return (a2r * a1r - a2i * a1i,
            a2r * a1i + a2i * a1r,
            a2r * b1r - a2i * b1i + b2r,
            a2r * b1i + a2i * b1r + b2i)


def s5_branch(u, h0_re, h0_im, p):
    b, l = u.shape[:2]
    ug = u.reshape(b, l, S5_GROUPS, S5_GROUP).astype(jnp.float32)
    ab_re, ab_im, bb_re, bb_im = s5_discretise(p['s5_log_dt'], p['s5_a_re'], p['s5_a_im'],
                                               p['s5_b_re'], p['s5_b_im'])
    bu_re = jnp.einsum('gpc,blgc->blgp', bb_re, ug)
    bu_im = jnp.einsum('gpc,blgc->blgp', bb_im, ug)
    if h0_re is not None:
        hr, hi = h0_re.astype(jnp.float32), h0_im.astype(jnp.float32)
        bu_re = bu_re.at[:, 0].add(ab_re * hr - ab_im * hi)
        bu_im = bu_im.at[:, 0].add(ab_re * hi + ab_im * hr)
    a_re = jnp.broadcast_to(ab_re, bu_re.shape)
    a_im = jnp.broadcast_to(ab_im, bu_im.shape)
    _, _, h_re, h_im = lax.associative_scan(s5_combine, (a_re, a_im, bu_re, bu_im), axis=1)
    y = (jnp.einsum('gcp,blgp->blgc', p['s5_c_re'].astype(jnp.float32), h_re)
         - jnp.einsum('gcp,blgp->blgc', p['s5_c_im'].astype(jnp.float32), h_im)
         + p['s5_d'].astype(jnp.float32) * ug)
    y = jax.nn.gelu(y.reshape(b, l, S5_WIDTH).astype(u.dtype))
    y = y * jax.nn.sigmoid(y @ p['w_glu'] + p['b_glu'])
    return y @ p['w_s5_proj'], h_re[:, -1].astype(u.dtype), h_im[:, -1].astype(u.dtype)


def sink_softmax(s, sink):
    m = jnp.maximum(jnp.max(s, axis=-1, keepdims=True), sink)
    e = jnp.exp(s - m)
    return e / (jnp.sum(e, axis=-1, keepdims=True) + jnp.exp(sink - m))


def banded_window_attention(q, k, v, sinks):
    b, l = q.shape[:2]
    nc = l // CHUNK
    span = (WIN_CHUNKS + 1) * CHUNK
    qc = q.reshape(b, nc, CHUNK, N_KV_HEADS, GQA_REP, HEAD_DIM)
    kc = k.reshape(b, nc, CHUNK, N_KV_HEADS, HEAD_DIM)
    vc = v.reshape(b, nc, CHUNK, N_KV_HEADS, HEAD_DIM)
    pad = ((0, 0), (WIN_CHUNKS, 0), (0, 0), (0, 0), (0, 0))
    kp, vp = jnp.pad(kc, pad), jnp.pad(vc, pad)
    kb = jnp.concatenate([kp[:, j:j + nc] for j in range(WIN_CHUNKS + 1)], axis=2)
    vb = jnp.concatenate([vp[:, j:j + nc] for j in range(WIN_CHUNKS + 1)], axis=2)
    s = jnp.einsum('bnqkrd,bnskd->bnkrqs', qc, kb, preferred_element_type=jnp.float32) * ATTN_SCALE
    key_chunk = jnp.arange(nc)[:, None] + (jnp.arange(span) // CHUNK - WIN_CHUNKS)[None, :]
    s = jnp.where((key_chunk >= 0)[None, :, None, None, None, :], s, NEG_INF)
    sink = sinks.astype(jnp.float32).reshape(1, 1, N_KV_HEADS, GQA_REP, 1, 1)
    w = sink_softmax(s, sink)
    o = jnp.einsum('bnkrqs,bnskd->bnqkrd', w.astype(v.dtype), vb)
    return o.reshape(b, l, Q_WIDTH), k[:, -WIN_CACHE:], v[:, -WIN_CACHE:]


def cached_window_attention(q, k, v, cache_k, cache_v, sinks):
    b, l = q.shape[:2]
    kk = jnp.concatenate([cache_k.astype(k.dtype), k], axis=1)
    vv = jnp.concatenate([cache_v.astype(v.dtype), v], axis=1)
    qg = q.reshape(b, l, N_KV_HEADS, GQA_REP, HEAD_DIM)
    s = jnp.einsum('bqkrd,bskd->bkrqs', qg, kk, preferred_element_type=jnp.float32) * ATTN_SCALE
    sink = sinks.astype(jnp.float32).reshape(1, N_KV_HEADS, GQA_REP, 1, 1)
    w = sink_softmax(s, sink)
    o = jnp.einsum('bkrqs,bskd->bqkrd', w.astype(vv.dtype), vv)
    return o.reshape(b, l, Q_WIDTH), kk[:, -WIN_CACHE:], vv[:, -WIN_CACHE:]


def trunk_layer(x, c, pos, p, cache_k=None, cache_v=None, h0_re=None, h0_im=None):
    mod = adaln_params(c, p['w_ada'], p['b_ada'])
    h = x + FFN_RES * mod[:, 0, 2] * swiglu(modulate(x, p['norm_ffn1'], mod[:, 0, 0], mod[:, 0, 1]),
                                            p['w_ffn1_in'], p['w_ffn1_out'])
    u = modulate(h, p['norm_mix'], mod[:, 1, 0], mod[:, 1, 1])
    s5_u, q, k, v, g_s5, g_attn = project_inputs(u, p['w_in'], p['q_norm'], p['k_norm'], pos)
    y_s5, s_re, s_im = s5_branch(s5_u, h0_re, h0_im, p)
    if cache_k is None:
        o, k_win, v_win = banded_window_attention(q, k, v, p['attn_sinks'])
    else:
        o, k_win, v_win = cached_window_attention(q, k, v, cache_k, cache_v, p['attn_sinks'])
    y_attn = o @ p['w_attn_proj']
    merged = g_s5 * y_s5 + g_attn * y_attn
    h = h + mod[:, 1, 2] * (merged @ p['w_out'])
    h = h + FFN_RES * mod[:, 2, 2] * swiglu(modulate(h, p['norm_ffn2'], mod[:, 2, 0], mod[:, 2, 1]),
                                            p['w_ffn2_in'], p['w_ffn2_out'])
    return rmsnorm(h, p['norm_final']), (k_win, v_win, s_re, s_im)


def setup_inputs(seed: int = 0) -> dict:
    key = jax.random.key(seed)
    ks = iter(jax.random.split(key, 48))

    def nrm(shape, scale):
        return jax.random.normal(next(ks), shape, jnp.float32) * scale

    def gain(shape):
        return 1.0 + nrm(shape, 0.02)

    L = (DEPTH,)
    G, P, GC = S5_GROUPS, S5_STATE, S5_GROUP
    x_prompt = nrm((BATCH, SEQ, D_MODEL), 1.0)
    x_sample = nrm((DEC_BATCH, DEC_SEQ, D_MODEL), 1.0)
    cache_k = nrm(L + (DEC_BATCH, WIN_CACHE, N_KV_HEADS, HEAD_DIM), 1.0)
    cache_v = nrm(L + (DEC_BATCH, WIN_CACHE, N_KV_HEADS, HEAD_DIM), 1.0)
    state_s5_re = nrm(L + (DEC_BATCH, G, P), 0.1)
    state_s5_im = nrm(L + (DEC_BATCH, G, P), 0.1)
    c_prompt = nrm((BATCH, D_MODEL), 1.0)
    c_sample = nrm((DEC_BATCH, D_MODEL), 1.0)
    n_idx = jnp.arange(P, dtype=jnp.float32)
    return {
        'x_prompt': x_prompt,
        'x_sample': x_sample,
        'cache_k': cache_k,
        'cache_v': cache_v,
        'state_s5_re': state_s5_re,
        'state_s5_im': state_s5_im,
        'c_prompt': c_prompt,
        'c_sample': c_sample,
        'w_ada': nrm(L + (D_MODEL, N_SUBLAYERS * 3 * D_MODEL), 0.3 * D_MODEL ** -0.5),
        'b_ada': nrm(L + (N_SUBLAYERS * 3 * D_MODEL,), 0.02),
        'norm_ffn1': gain(L + (D_MODEL,)),
        'w_ffn1_in': nrm(L + (D_MODEL, 2 * FF_DIM), D_MODEL ** -0.5),
        'w_ffn1_out': nrm(L + (FF_DIM, D_MODEL), FF_DIM ** -0.5),
        'norm_mix': gain(L + (D_MODEL,)),
        'w_in': nrm(L + (D_MODEL, IN_WIDTH), D_MODEL ** -0.5),
        's5_log_dt': jax.random.uniform(next(ks), L + (G,), jnp.float32, math.log(1e-3), math.log(1e-1)),
        's5_a_re': -0.5 + nrm(L + (G, P), 0.01),
        's5_a_im': math.pi * n_idx + nrm(L + (G, P), 0.01),
        's5_b_re': nrm(L + (G, P, GC), (2.0 * GC) ** -0.5),
        's5_b_im': nrm(L + (G, P, GC), (2.0 * GC) ** -0.5),
        's5_c_re': nrm(L + (G, GC, P), P ** -0.5),
        's5_c_im': nrm(L + (G, GC, P), P ** -0.5),
        's5_d': nrm(L + (G, GC), 1.0),
        'w_glu': nrm(L + (S5_WIDTH, S5_WIDTH), S5_WIDTH ** -0.5),
        'b_glu': nrm(L + (S5_WIDTH,), 0.02),
        'w_s5_proj': nrm(L + (S5_WIDTH, D_MODEL), S5_WIDTH ** -0.5),
        'q_norm': gain(L + (HEAD_DIM,)),
        'k_norm': gain(L + (HEAD_DIM,)),
        'attn_sinks': nrm(L + (N_HEADS,), 0.5),
        'w_attn_proj': nrm(L + (Q_WIDTH, D_MODEL), Q_WIDTH ** -0.5),
        'w_out': nrm(L + (D_MODEL, D_MODEL), D_MODEL ** -0.5),
        'norm_ffn2': gain(L + (D_MODEL,)),
        'w_ffn2_in': nrm(L + (D_MODEL, 2 * FF_DIM), D_MODEL ** -0.5),
        'w_ffn2_out': nrm(L + (FF_DIM, D_MODEL), FF_DIM ** -0.5),
        'norm_final': gain(L + (D_MODEL,)),
    }


def reference(x_prompt, x_sample, cache_k, cache_v, state_s5_re, state_s5_im, c_prompt, c_sample,
              w_ada, b_ada, norm_ffn1, w_ffn1_in, w_ffn1_out, norm_mix, w_in,
              s5_log_dt, s5_a_re, s5_a_im, s5_b_re, s5_b_im, s5_c_re, s5_c_im, s5_d,
              w_glu, b_glu, w_s5_proj, q_norm, k_norm, attn_sinks, w_attn_proj, w_out,
              norm_ffn2, w_ffn2_in, w_ffn2_out, norm_final):
    pos_prompt = jnp.arange(x_prompt.shape[1])
    pos_sample = PAST_LEN + jnp.arange(x_sample.shape[1])
    y_prompt, y_sample = x_prompt, x_sample
    st_prompt, st_sample = [], []
    for l in range(DEPTH):
        p = {
            'w_ada': w_ada[l], 'b_ada': b_ada[l], 'norm_ffn1': norm_ffn1[l],
            'w_ffn1_in': w_ffn1_in[l], 'w_ffn1_out': w_ffn1_out[l], 'norm_mix': norm_mix[l],
            'w_in': w_in[l], 's5_log_dt': s5_log_dt[l], 's5_a_re': s5_a_re[l], 's5_a_im': s5_a_im[l],
            's5_b_re': s5_b_re[l], 's5_b_im': s5_b_im[l], 's5_c_re': s5_c_re[l], 's5_c_im': s5_c_im[l],
            's5_d': s5_d[l], 'w_glu': w_glu[l], 'b_glu': b_glu[l], 'w_s5_proj': w_s5_proj[l],
            'q_norm': q_norm[l], 'k_norm': k_norm[l], 'attn_sinks': attn_sinks[l],
            'w_attn_proj': w_attn_proj[l], 'w_out': w_out[l], 'norm_ffn2': norm_ffn2[l],
            'w_ffn2_in': w_ffn2_in[l], 'w_ffn2_out': w_ffn2_out[l], 'norm_final': norm_final[l],
        }
        y_prompt, sp = trunk_layer(y_prompt, c_prompt, pos_prompt, p)
        y_sample, ss = trunk_layer(y_sample, c_sample, pos_sample, p,
                                   cache_k[l], cache_v[l], state_s5_re[l], state_s5_im[l])
        st_prompt.append(sp)
        st_sample.append(ss)
    k_win_prompt = jnp.stack([s[0] for s in st_prompt])
    v_win_prompt = jnp.stack([s[1] for s in st_prompt])
    s5_re_prompt = jnp.stack([s[2] for s in st_prompt])
    s5_im_prompt = jnp.stack([s[3] for s in st_prompt])
    k_win_sample = jnp.stack([s[0] for s in st_sample])
    v_win_sample = jnp.stack([s[1] for s in st_sample])
    s5_re_sample = jnp.stack([s[2] for s in st_sample])
    s5_im_sample = jnp.stack([s[3] for s in st_sample])
    return (y_prompt, y_sample, k_win_prompt, v_win_prompt, s5_re_prompt, s5_im_prompt,
            k_win_sample, v_win_sample, s5_re_sample, s5_im_sample)
```

```python
import functools
import math

import jax
import jax.numpy as jnp
from jax import lax
from jax.experimental import pallas as pl
from jax.experimental.pallas import tpu as pltpu

F32 = jnp.float32
BF16 = jnp.bfloat16

D_MODEL = 1024
FF_DIM = 2816
N_SUBLAYERS = 3
FFN_RES = 0.5
RMS_EPS = 1e-6
CHUNK = 64
PAST_LEN = 2048
S5_WIDTH = 512
S5_GROUP = 16
S5_GROUPS = 32
S5_STATE = 64
HEAD_DIM = 64
N_HEADS = 8
N_KV_HEADS = 2
GQA_REP = N_HEADS // N_KV_HEADS
WINDOW = 128
WIN_CHUNKS = WINDOW // CHUNK
ROPE_THETA = 500000.0
ROPE_DIM = HEAD_DIM // 4
ATTN_SCALE = HEAD_DIM ** -0.5
NEG_INF = -1e30
Q_WIDTH = N_HEADS * HEAD_DIM
KV_WIDTH = N_KV_HEADS * HEAD_DIM
QKV_END = S5_WIDTH + Q_WIDTH + 2 * KV_WIDTH
IN_WIDTH = QKV_END + 2 * D_MODEL

V7X_LANES = 128
V7X_SUBLANES = 8
V7X_VMEM_BYTES = 64 * 1024 * 1024

TILE = 512
SEG = TILE // V7X_SUBLANES
FF_COLS = 256
S5_BLOCKS = 4
S5_BLOCK_GROUPS = S5_GROUPS // S5_BLOCKS
S5_BLOCK_STATES = S5_BLOCK_GROUPS * S5_STATE
S5_BLOCK_IN = S5_BLOCK_GROUPS * S5_GROUP
STATE_COLS = S5_BLOCKS * 2 * S5_BLOCK_STATES
KV_SPAN = (WIN_CHUNKS + 1) * CHUNK


def _dot(a, b):
    return jnp.dot(a, b, preferred_element_type=F32)


def _rms(x):
    return x * lax.rsqrt(jnp.mean(x * x, axis=-1, keepdims=True) + RMS_EPS)


def _per_seq(v, rows):
    nb, d = v.shape
    if nb == 1:
        return v
    return jnp.broadcast_to(v[:, None, :], (nb, rows // nb, d)).reshape(rows, d)


def _modulated(x, norm_w, shift, scale):
    rows = x.shape[0]
    return (_rms(x) * norm_w) * (1.0 + _per_seq(scale, rows)) + _per_seq(shift, rows)


def _ada_kernel(c_ref, w_ref, b_ref, o_ref):
    c = c_ref[...]
    s = (c * jax.nn.sigmoid(c)).astype(BF16)
    o_ref[...] = _dot(s, w_ref[...].astype(BF16)) + b_ref[...]


def _ada_call(c_pad, w_ada, b_ada):
    rows = c_pad.shape[0]
    n = w_ada.shape[1]
    bn = 1024
    return pl.pallas_call(
        _ada_kernel,
        grid=(n // bn,),
        in_specs=[pl.BlockSpec((rows, D_MODEL), lambda i: (0, 0)),
                  pl.BlockSpec((D_MODEL, bn), lambda i: (0, i)),
                  pl.BlockSpec((1, bn), lambda i: (0, i))],
        out_specs=pl.BlockSpec((rows, bn), lambda i: (0, i)),
        out_shape=jax.ShapeDtypeStruct((rows, n), F32),
        compiler_params=pltpu.CompilerParams(dimension_semantics=("arbitrary",)),
        name="adaln",
    )(c_pad, w_ada, b_ada)


def _s5_prep_kernel(ldt_ref, are_ref, aim_ref, bre_ref, bim_ref, bbre_ref, bbim_ref, pre_ref, pim_ref):
    dt = jnp.exp(ldt_ref[...])
    lr = jnp.minimum(are_ref[...], -1e-4)
    li = aim_ref[...]
    mag = jnp.exp(lr * dt)
    ab_re = mag * jnp.cos(li * dt)
    ab_im = mag * jnp.sin(li * dt)
    nr, ni = ab_re - 1.0, ab_im
    den = lr * lr + li * li
    f_re = (nr * lr + ni * li) / den
    f_im = (ni * lr - nr * li) / den
    for g in range(S5_GROUPS):
        fr, fi = f_re[g:g + 1, :], f_im[g:g + 1, :]
        br, bi = bre_ref[g], bim_ref[g]
        bbre_ref[g] = fr * br - fi * bi
        bbim_ref[g] = fr * bi + fi * br
    pr, pi = ab_re, ab_im
    for j in range(SEG):
        pre_ref[j] = pr
        pim_ref[j] = pi
        pr, pi = pr * ab_re - pi * ab_im, pr * ab_im + pi * ab_re


def _s5_prep_call(log_dt, a_re, a_im, b_re_t, b_im_t):
    g, p, gc = S5_GROUPS, S5_STATE, S5_GROUP
    return pl.pallas_call(
        _s5_prep_kernel,
        out_shape=(jax.ShapeDtypeStruct((g, gc, p), F32), jax.ShapeDtypeStruct((g, gc, p), F32),
                   jax.ShapeDtypeStruct((SEG, g, p), F32), jax.ShapeDtypeStruct((SEG, g, p), F32)),
        name="s5_discretise",
    )(log_dt, a_re, a_im, b_re_t, b_im_t)


def _ffn_kernel(*refs, sub, final_norm):
    if final_norm:
        x_ref, mod_ref, nw_ref, w1_ref, w2_ref, nf_ref, o_ref, hid_ref = refs
    else:
        x_ref, mod_ref, nw_ref, w1_ref, w2_ref, o_ref, hid_ref = refs
    x = x_ref[...]
    rows = x.shape[0]
    shift, scale, gate = mod_ref[:, 3 * sub, :], mod_ref[:, 3 * sub + 1, :], mod_ref[:, 3 * sub + 2, :]
    u = _modulated(x, nw_ref[...], shift, scale).astype(BF16)
    for c in range(FF_DIM // FF_COLS):
        lo = c * FF_COLS
        a = _dot(u, w1_ref[:, lo:lo + FF_COLS])
        b = _dot(u, w1_ref[:, FF_DIM + lo:FF_DIM + lo + FF_COLS])
        hid_ref[:, lo:lo + FF_COLS] = (a * jax.nn.sigmoid(a) * b).astype(BF16)
    y = _dot(hid_ref[...], w2_ref[...])
    out = x + _per_seq(FFN_RES * gate, rows) * y
    if final_norm:
        out = _rms(out) * nf_ref[...]
    o_ref[...] = out


def _resident(shape):
    zeros = (0,) * len(shape)
    return pl.BlockSpec(shape, lambda i: zeros, pipeline_mode=pl.Buffered(1))


def _ffn_call(x2d, mod, norm_w, w1, w2, norm_final, *, nb, tiles_per_mod, sub, name):
    n = x2d.shape[0]
    final_norm = norm_final is not None
    in_specs = [pl.BlockSpec((TILE, D_MODEL), lambda i: (i, 0)),
                pl.BlockSpec((nb, 3 * N_SUBLAYERS, D_MODEL), lambda i: (i // tiles_per_mod, 0, 0)),
                _resident((1, D_MODEL)),
                _resident((D_MODEL, 2 * FF_DIM)),
                _resident((FF_DIM, D_MODEL))]
    args = [x2d, mod, norm_w, w1, w2]
    if final_norm:
        in_specs.append(_resident((1, D_MODEL)))
        args.append(norm_final)
    return pl.pallas_call(
        functools.partial(_ffn_kernel, sub=sub, final_norm=final_norm),
        grid=(n // TILE,),
        in_specs=in_specs,
        out_specs=pl.BlockSpec((TILE, D_MODEL), lambda i: (i, 0)),
        out_shape=jax.ShapeDtypeStruct((n, D_MODEL), F32),
        scratch_shapes=[pltpu.VMEM((TILE, FF_DIM), BF16)],
        compiler_params=pltpu.CompilerParams(dimension_semantics=("arbitrary",),
                                             vmem_limit_bytes=48 * 1024 * 1024),
        name=name,
    )(*args)


def _mixer_kernel(*refs, prompt, tiles_per_seq):
    (h_ref, mod_ref, nmix_ref, win_ref, qn_ref, kn_ref, ones_ref, invf_ref, m1_ref, m2_ref, esel_ref,
     sinks_ref, bbd_ref, cbd_ref, apow_ref, dvec_ref, wglu_ref, bglu_ref, ws5_ref, wattn_ref,
     wout_ref) = refs[:21]
    if prompt:
        (h2_ref, kout_ref, vout_ref, stout_ref,
         us_ref, perm_ref, bu_ref, yperm_ref, qs_ref, kt_ref, vt_ref, os_ref,
         carry_ref, kcar_ref, vcar_ref) = refs[21:]
    else:
        (ck_ref, cv_ref, stin_ref,
         h2_ref, kout_ref, vout_ref, stout_ref,
         us_ref, perm_ref, bu_ref, yperm_ref, qs_ref, kt_ref, vt_ref, os_ref,
         ktc_ref, vtc_ref) = refs[21:]

    i = pl.program_id(0)
    tile_in_seq = i % tiles_per_seq
    rows = h_ref.shape[0]
    n_chunks = rows // CHUNK
    h = h_ref[...]
    shift, scale, gate = mod_ref[:, 3, :], mod_ref[:, 4, :], mod_ref[:, 5, :]
    u = _modulated(h, nmix_ref[...], shift, scale).astype(BF16)

    s5_u = _dot(u, win_ref[:, 0:S5_WIDTH])
    for b in range(S5_BLOCKS):
        us_ref[b] = s5_u[:, b * S5_BLOCK_IN:(b + 1) * S5_BLOCK_IN]
    q_raw = _dot(u, win_ref[:, S5_WIDTH:S5_WIDTH + Q_WIDTH])
    kv_raw = _dot(u, win_ref[:, S5_WIDTH + Q_WIDTH:QKV_END])
    k_raw, v_new = kv_raw[:, :KV_WIDTH], kv_raw[:, KV_WIDTH:]

    def head_rms(x, w_row, ones):
        sq = x * x
        hi = sq.astype(BF16)
        lo = (sq - hi.astype(F32)).astype(BF16)
        ms = (_dot(hi, ones) + _dot(lo, ones)) * (1.0 / HEAD_DIM)
        return x * lax.rsqrt(ms + RMS_EPS) * w_row

    row_id = lax.broadcasted_iota(jnp.int32, (rows, V7X_LANES), 0)
    if prompt:
        pos = tile_in_seq * rows + row_id
    else:
        pos = PAST_LEN + (row_id & (CHUNK - 1))
    ang = pos.astype(F32) * invf_ref[...]
    cs, sn = jnp.cos(ang), jnp.sin(ang)
    s_up, s_dn = sn * m1_ref[...], sn * m2_ref[...]

    def rope(xt):
        return (xt * cs + pltpu.roll(xt, V7X_LANES - ROPE_DIM // 2, 1) * s_up
                + pltpu.roll(xt, ROPE_DIM // 2, 1) * s_dn)

    qn = head_rms(q_raw, qn_ref[...], ones_ref[...])
    kn = head_rms(k_raw, kn_ref[...], ones_ref[0:KV_WIDTH, 0:KV_WIDTH])
    q = jnp.concatenate([rope(qn[:, t * V7X_LANES:(t + 1) * V7X_LANES])
                         for t in range(Q_WIDTH // V7X_LANES)], axis=1)
    k_new = rope(kn)
    qs_ref[...] = (q * ATTN_SCALE).astype(BF16)
    kout_ref[...] = k_new
    vout_ref[...] = v_new

    head_of_lane = lax.broadcasted_iota(jnp.int32, (1, GQA_REP * HEAD_DIM), 1) // HEAD_DIM
    if prompt:
        @pl.when(tile_in_seq == 0)
        def _():
            kcar_ref[...] = jnp.zeros_like(kcar_ref)
            vcar_ref[...] = jnp.zeros_like(vcar_ref)
            carry_ref[...] = jnp.zeros_like(carry_ref)
        k_ext = jnp.concatenate([kcar_ref[...], k_new], axis=0).astype(BF16)
        v_ext = jnp.concatenate([vcar_ref[...], v_new], axis=0).astype(BF16)
        kcar_ref[...] = k_new[rows - WINDOW:, :]
        vcar_ref[...] = v_new[rows - WINDOW:, :]
        key_chunk = lax.broadcasted_iota(jnp.int32, (1, KV_SPAN), 1) // CHUNK
    else:
        k_ext, v_ext = k_new.astype(BF16), v_new.astype(BF16)
        k_cache, v_cache = ck_ref[...].astype(BF16), cv_ref[...].astype(BF16)

    for j in range(N_KV_HEADS):
        sel = esel_ref[j]
        n_ext = k_ext.shape[0]
        kt_ref[0:n_ext, :] = _dot(k_ext, sel).astype(BF16)
        vt_ref[0:n_ext, :] = _dot(v_ext, sel).astype(BF16)
        if not prompt:
            ktc_ref[...] = _dot(k_cache, sel).astype(BF16)
            vtc_ref[...] = _dot(v_cache, sel).astype(BF16)
        sink_col = jnp.concatenate(
            [jnp.full((CHUNK, 1), sinks_ref[GQA_REP * j + r], F32) for r in range(GQA_REP)], axis=0)
        lo = j * GQA_REP * HEAD_DIM

        def chunk_body(c, carry, lo=lo, sink_col=sink_col):
            r0 = pl.multiple_of(c * CHUNK, CHUNK)
            qc = qs_ref[pl.ds(r0, CHUNK), lo:lo + GQA_REP * HEAD_DIM]
            qm = jnp.concatenate([jnp.where(head_of_lane == r, qc, jnp.zeros_like(qc))
                                  for r in range(GQA_REP)], axis=0)
            if prompt:
                keys = kt_ref[pl.ds(r0, KV_SPAN), :]
                vals = vt_ref[pl.ds(r0, KV_SPAN), :]
            else:
                c0 = pl.multiple_of(c * WINDOW, WINDOW)
                keys = jnp.concatenate([ktc_ref[pl.ds(c0, WINDOW), :], kt_ref[pl.ds(r0, CHUNK), :]], axis=0)
                vals = jnp.concatenate([vtc_ref[pl.ds(c0, WINDOW), :], vt_ref[pl.ds(r0, CHUNK), :]], axis=0)
            s = lax.dot_general(qm, keys, (((1,), (1,)), ((), ())), preferred_element_type=F32)
            if prompt:
                first_chunk = tile_in_seq * n_chunks + c - WIN_CHUNKS
                s = jnp.where(first_chunk + key_chunk >= 0, s, NEG_INF)
            m = jnp.maximum(jnp.max(s, axis=1, keepdims=True), sink_col)
            e = jnp.exp(s - m)
            den = jnp.sum(e, axis=1, keepdims=True) + jnp.exp(sink_col - m)
            pv = _dot(e.astype(BF16), vals) * (1.0 / den)
            oc = jnp.zeros((CHUNK, GQA_REP * HEAD_DIM), F32)
            for r in range(GQA_REP):
                oc = oc + jnp.where(head_of_lane == r, pv[r * CHUNK:(r + 1) * CHUNK, :], 0.0)
            os_ref[pl.ds(r0, CHUNK), lo:lo + GQA_REP * HEAD_DIM] = oc.astype(BF16)
            return carry

        lax.fori_loop(0, n_chunks, chunk_body, 0)

    ns = S5_BLOCK_STATES
    groups_per_seg = SEG // V7X_SUBLANES
    for b in range(S5_BLOCKS):
        c_re, c_im = 2 * ns * b, 2 * ns * b + ns
        for j in range(SEG):
            perm_ref[b, V7X_SUBLANES * j:V7X_SUBLANES * (j + 1), :] = (
                us_ref.at[b][pl.ds(j, V7X_SUBLANES, stride=SEG), :])
        bu_ref[...] = _dot(perm_ref[b].astype(BF16), bbd_ref[b]).reshape(SEG, V7X_SUBLANES, 2 * ns)
        a_re = jnp.broadcast_to(apow_ref[0:1, c_re:c_re + ns], (V7X_SUBLANES, ns))
        a_im = jnp.broadcast_to(apow_ref[0:1, c_im:c_im + ns], (V7X_SUBLANES, ns))

        def scan_step(j, hc, a_re=a_re, a_im=a_im):
            hr, hi = hc
            nr = a_re * hr - a_im * hi + bu_ref[j, :, 0:ns]
            ni = a_re * hi + a_im * hr + bu_ref[j, :, ns:2 * ns]
            bu_ref[j, :, 0:ns] = nr
            bu_ref[j, :, ns:2 * ns] = ni
            return nr, ni

        zero = jnp.zeros((V7X_SUBLANES, ns), F32)
        end_re, end_im = lax.fori_loop(0, SEG, scan_step, (zero, zero), unroll=4)

        if prompt:
            p_re, p_im = apow_ref[SEG - 1:SEG, c_re:c_re + ns], apow_ref[SEG - 1:SEG, c_im:c_im + ns]
            s_re, s_im = carry_ref[0:1, c_re:c_re + ns], carry_ref[0:1, c_im:c_im + ns]
            in_re, in_im = [], []
            for s in range(V7X_SUBLANES):
                in_re.append(s_re)
                in_im.append(s_im)
                s_re, s_im = (p_re * s_re - p_im * s_im + end_re[s:s + 1, :],
                              p_re * s_im + p_im * s_re + end_im[s:s + 1, :])
            cin_re, cin_im = jnp.concatenate(in_re, axis=0), jnp.concatenate(in_im, axis=0)
            carry_ref[0:1, c_re:c_re + ns] = s_re
            carry_ref[0:1, c_im:c_im + ns] = s_im
            stout_ref[0, :, c_re:c_re + ns] = s_re
            stout_ref[0, :, c_im:c_im + ns] = s_im
        else:
            cin_re, cin_im = stin_ref[0, :, c_re:c_re + ns], stin_ref[0, :, c_im:c_im + ns]

        def fix_step(j, carry, cin_re=cin_re, cin_im=cin_im, c_re=c_re, c_im=c_im):
            pr = apow_ref[pl.ds(j, 1), c_re:c_re + ns]
            pi = apow_ref[pl.ds(j, 1), c_im:c_im + ns]
            bu_ref[j, :, 0:ns] = bu_ref[j, :, 0:ns] + (pr * cin_re - pi * cin_im)
            bu_ref[j, :, ns:2 * ns] = bu_ref[j, :, ns:2 * ns] + (pr * cin_im + pi * cin_re)
            return carry

        lax.fori_loop(0, SEG, fix_step, 0, unroll=4)
        if not prompt:
            stout_ref[0, :, c_re:c_re + ns] = bu_ref[SEG - 1, :, 0:ns]
            stout_ref[0, :, c_im:c_im + ns] = bu_ref[SEG - 1, :, ns:2 * ns]
        hb = bu_ref[...].reshape(rows, 2 * ns).astype(BF16)
        yperm_ref[b] = _dot(hb, cbd_ref[b])
        d_row = dvec_ref[:, b * S5_BLOCK_IN:(b + 1) * S5_BLOCK_IN]
        for g in range(rows // V7X_SUBLANES):
            s, j0 = g // groups_per_seg, V7X_SUBLANES * (g % groups_per_seg)
            r0 = V7X_SUBLANES * g
            perm_ref[b, r0:r0 + V7X_SUBLANES, :] = (
                yperm_ref.at[b][pl.ds(V7X_SUBLANES * j0 + s, V7X_SUBLANES, stride=V7X_SUBLANES), :]
                + d_row * us_ref[b, r0:r0 + V7X_SUBLANES, :])
    y = jax.nn.gelu(jnp.concatenate([perm_ref[b] for b in range(S5_BLOCKS)], axis=1))
    y = y * jax.nn.sigmoid(_dot(y.astype(BF16), wglu_ref[...]) + bglu_ref[...])
    y_s5 = _dot(y.astype(BF16), ws5_ref[...])

    y_attn = _dot(os_ref[...], wattn_ref[...])
    g_s5 = jax.nn.sigmoid(_dot(u, win_ref[:, QKV_END:QKV_END + D_MODEL]))
    g_attn = jax.nn.sigmoid(_dot(u, win_ref[:, QKV_END + D_MODEL:IN_WIDTH]))
    merged = (g_s5 * y_s5 + g_attn * y_attn).astype(BF16)
    h2_ref[...] = h + _per_seq(gate, rows) * _dot(merged, wout_ref[...])


def _mixer_call(h2d, mod, consts, *, prompt, nb, tiles_per_seq, cache=None):
    n = h2d.shape[0]
    n_seq_blocks = n // (TILE * tiles_per_seq)
    state_rows = 1 if prompt else V7X_SUBLANES
    smem = pl.BlockSpec(memory_space=pltpu.SMEM)
    in_specs = [pl.BlockSpec((TILE, D_MODEL), lambda i: (i, 0)),
                pl.BlockSpec((nb, 3 * N_SUBLAYERS, D_MODEL), lambda i: (i // tiles_per_seq, 0, 0))]
    args = [h2d, mod]
    for name in ("nmix", "win", "qn", "kn", "ones", "invf", "m1", "m2", "esel"):
        in_specs.append(_resident(consts[name].shape))
        args.append(consts[name])
    in_specs.append(smem)
    args.append(consts["sinks"])
    for name in ("bbd", "cbd", "apow", "dvec", "wglu", "bglu", "ws5", "wattn", "wout"):
        in_specs.append(_resident(consts[name].shape))
        args.append(consts[name])
    if not prompt:
        ck, cv, st_in = cache
        in_specs += [_resident(ck.shape), _resident(cv.shape), _resident(st_in.shape)]
        args += [ck, cv, st_in]
    out_specs = (pl.BlockSpec((TILE, D_MODEL), lambda i: (i, 0)),
                 pl.BlockSpec((TILE, KV_WIDTH), lambda i: (i, 0)),
                 pl.BlockSpec((TILE, KV_WIDTH), lambda i: (i, 0)),
                 pl.BlockSpec((1, state_rows, STATE_COLS), lambda i: (i // tiles_per_seq, 0, 0)))
    out_shape = (jax.ShapeDtypeStruct((n, D_MODEL), F32),
                 jax.ShapeDtypeStruct((n, KV_WIDTH), F32),
                 jax.ShapeDtypeStruct((n, KV_WIDTH), F32),
                 jax.ShapeDtypeStruct((n_seq_blocks, state_rows, STATE_COLS), F32))
    s5_tiles = (S5_BLOCKS, TILE, S5_BLOCK_IN)
    scratch = [pltpu.VMEM(s5_tiles, F32),
               pltpu.VMEM(s5_tiles, F32),
               pltpu.VMEM((SEG, V7X_SUBLANES, 2 * S5_BLOCK_STATES), F32),
               pltpu.VMEM(s5_tiles, F32),
               pltpu.VMEM((TILE, Q_WIDTH), BF16),
               pltpu.VMEM((TILE + WINDOW, GQA_REP * HEAD_DIM), BF16),
               pltpu.VMEM((TILE + WINDOW, GQA_REP * HEAD_DIM), BF16),
               pltpu.VMEM((TILE, Q_WIDTH), BF16)]
    if prompt:
        scratch += [pltpu.VMEM((1, STATE_COLS), F32),
                    pltpu.VMEM((WINDOW, KV_WIDTH), F32),
                    pltpu.VMEM((WINDOW, KV_WIDTH), F32)]
    else:
        scratch += [pltpu.VMEM((V7X_SUBLANES * WINDOW, GQA_REP * HEAD_DIM), BF16),
                    pltpu.VMEM((V7X_SUBLANES * WINDOW, GQA_REP * HEAD_DIM), BF16)]
    return pl.pallas_call(
        functools.partial(_mixer_kernel, prompt=prompt, tiles_per_seq=tiles_per_seq),
        grid=(n // TILE,),
        in_specs=in_specs,
        out_specs=out_specs,
        out_shape=out_shape,
        scratch_shapes=scratch,
        compiler_params=pltpu.CompilerParams(dimension_semantics=("arbitrary",),
                                             vmem_limit_bytes=56 * 1024 * 1024),
        name="mixer_prompt" if prompt else "mixer_sample",
    )(*args)


def _block_layout(re, im):
    lead = re.shape[:-2]
    re = re.reshape(lead + (S5_BLOCKS, S5_BLOCK_STATES))
    im = im.reshape(lead + (S5_BLOCKS, S5_BLOCK_STATES))
    return jnp.stack([re, im], axis=-2).reshape(lead + (STATE_COLS,))


def _from_block_layout(x):
    lead = x.shape[:-1]
    x = x.reshape(lead + (S5_BLOCKS, 2, S5_BLOCK_STATES))
    shape = lead + (S5_GROUPS, S5_STATE)
    return x[..., 0, :].reshape(shape), x[..., 1, :].reshape(shape)


def _mixer_constants(l, w_in, s5_log_dt, s5_a_re, s5_a_im, s5_b_re, s5_b_im, s5_c_re, s5_c_im, s5_d,
                     w_glu, b_glu, w_s5_proj, q_norm, k_norm, attn_sinks, w_attn_proj, w_out, norm_mix):
    bb_re, bb_im, pow_re, pow_im = _s5_prep_call(
        s5_log_dt[l][:, None], s5_a_re[l], s5_a_im[l],
        jnp.swapaxes(s5_b_re[l], 1, 2), jnp.swapaxes(s5_b_im[l], 1, 2))
    eye = jnp.eye(S5_BLOCK_GROUPS, dtype=F32)

    def in_blocks(x):
        x = x.reshape(S5_BLOCKS, S5_BLOCK_GROUPS, S5_GROUP, S5_STATE)
        return jnp.einsum("bgcp,gh->bgchp", x, eye).reshape(S5_BLOCKS, S5_BLOCK_IN, S5_BLOCK_STATES)

    def out_blocks(x):
        x = x.reshape(S5_BLOCKS, S5_BLOCK_GROUPS, S5_GROUP, S5_STATE)
        return jnp.einsum("bgcp,gh->bgphc", x, eye).reshape(S5_BLOCKS, S5_BLOCK_STATES, S5_BLOCK_IN)

    lane = jnp.arange(V7X_LANES)
    d = lane % HEAD_DIM
    half = ROPE_DIM // 2
    inv_freq = ROPE_THETA ** (-jnp.arange(half, dtype=F32) / half)
    head_rep = jnp.tile(jnp.eye(HEAD_DIM, dtype=F32), (1, GQA_REP))
    esel = jnp.stack([jnp.zeros((KV_WIDTH, GQA_REP * HEAD_DIM), F32).at[j * HEAD_DIM:(j + 1) * HEAD_DIM].set(head_rep)
                      for j in range(N_KV_HEADS)])
    return {
        "nmix": norm_mix[l][None],
        "win": w_in[l].astype(BF16),
        "qn": jnp.tile(q_norm[l], N_HEADS)[None],
        "kn": jnp.tile(k_norm[l], N_KV_HEADS)[None],
        "ones": jnp.kron(jnp.eye(N_HEADS, dtype=F32), jnp.ones((HEAD_DIM, HEAD_DIM), F32)).astype(BF16),
        "invf": jnp.where(d < ROPE_DIM, inv_freq[d % half], 0.0)[None].astype(F32),
        "m1": jnp.where(d < half, -1.0, 0.0)[None].astype(F32),
        "m2": jnp.where((d >= half) & (d < ROPE_DIM), 1.0, 0.0)[None].astype(F32),
        "esel": esel.astype(BF16),
        "sinks": attn_sinks[l],
        "bbd": jnp.concatenate([in_blocks(bb_re), in_blocks(bb_im)], axis=-1).astype(BF16),
        "cbd": jnp.concatenate([out_blocks(s5_c_re[l]), -out_blocks(s5_c_im[l])], axis=1).astype(BF16),
        "apow": _block_layout(pow_re, pow_im),
        "dvec": s5_d[l].reshape(1, S5_WIDTH),
        "wglu": w_glu[l].astype(BF16),
        "bglu": b_glu[l][None],
        "ws5": w_s5_proj[l].astype(BF16),
        "wattn": w_attn_proj[l].astype(BF16),
        "wout": w_out[l].astype(BF16),
    }


def kernel(x_prompt, x_sample, cache_k, cache_v, state_s5_re, state_s5_im, c_prompt, c_sample, w_ada, b_ada, norm_ffn1, w_ffn1_in, w_ffn1_out, norm_mix, w_in, s5_log_dt, s5_a_re, s5_a_im, s5_b_re, s5_b_im, s5_c_re, s5_c_im, s5_d, w_glu, b_glu, w_s5_proj, q_norm, k_norm, attn_sinks, w_attn_proj, w_out, norm_ffn2, w_ffn2_in, w_ffn2_out, norm_final):
    depth = w_ada.shape[0]
    bp, lp, _ = x_prompt.shape
    bs, ls, _ = x_sample.shape
    assert depth == 1 and lp % TILE == 0 and bs * ls == TILE and ls == CHUNK and bs == V7X_SUBLANES
    tiles_per_seq = lp // TILE
    l = 0

    c_all = jnp.concatenate([c_prompt, c_sample], axis=0)
    pad = (-c_all.shape[0]) % (2 * V7X_SUBLANES)
    c_pad = jnp.pad(c_all, ((0, pad), (0, 0)))
    mod = _ada_call(c_pad, w_ada[l], b_ada[l][None]).reshape(c_pad.shape[0], 3 * N_SUBLAYERS, D_MODEL)
    mod_p, mod_s = mod[:bp], mod[bp:bp + bs]

    consts = _mixer_constants(l, w_in, s5_log_dt, s5_a_re, s5_a_im, s5_b_re, s5_b_im, s5_c_re, s5_c_im,
                              s5_d, w_glu, b_glu, w_s5_proj, q_norm, k_norm, attn_sinks, w_attn_proj,
                              w_out, norm_mix)
    w1a, w1b = w_ffn1_in[l].astype(BF16), w_ffn1_out[l].astype(BF16)
    w2a, w2b = w_ffn2_in[l].astype(BF16), w_ffn2_out[l].astype(BF16)
    n1, n2, nf = norm_ffn1[l][None], norm_ffn2[l][None], norm_final[l][None]

    xp = x_prompt.reshape(bp * lp, D_MODEL)
    xs = x_sample.reshape(bs * ls, D_MODEL)
    cache = (cache_k[l].reshape(bs * WINDOW, KV_WIDTH), cache_v[l].reshape(bs * WINDOW, KV_WIDTH),
             _block_layout(state_s5_re[l], state_s5_im[l])[None])

    hp = _ffn_call(xp, mod_p, n1, w1a, w1b, None, nb=1, tiles_per_mod=tiles_per_seq, sub=0, name="ffn1_prompt")
    hs = _ffn_call(xs, mod_s, n1, w1a, w1b, None, nb=bs, tiles_per_mod=1, sub=0, name="ffn1_sample")
    hp, kp, vp, stp = _mixer_call(hp, mod_p, consts, prompt=True, nb=1, tiles_per_seq=tiles_per_seq)
    hs, ks, vs, sts = _mixer_call(hs, mod_s, consts, prompt=False, nb=bs, tiles_per_seq=1, cache=cache)
    yp = _ffn_call(hp, mod_p, n2, w2a, w2b, nf, nb=1, tiles_per_mod=tiles_per_seq, sub=2, name="ffn2_prompt")
    ys = _ffn_call(hs, mod_s, n2, w2a, w2b, nf, nb=bs, tiles_per_mod=1, sub=2, name="ffn2_sample")

    kv_shape = (N_KV_HEADS, HEAD_DIM)
    k_win_p = kp.reshape((bp, lp) + kv_shape)[:, lp - WINDOW:][None]
    v_win_p = vp.reshape((bp, lp) + kv_shape)[:, lp - WINDOW:][None]
    s5_re_p, s5_im_p = _from_block_layout(stp[:, 0, :])
    k_win_s = jnp.concatenate([cache_k[l][:, ls:], ks.reshape((bs, ls) + kv_shape)], axis=1)[None]
    v_win_s = jnp.concatenate([cache_v[l][:, ls:], vs.reshape((bs, ls) + kv_shape)], axis=1)[None]
    s5_re_s, s5_im_s = _from_block_layout(sts[0])
    return (yp.reshape(bp, lp, D_MODEL), ys.reshape(bs, ls, D_MODEL), k_win_p, v_win_p,
            s5_re_p[None], s5_im_p[None], k_win_s, v_win_s, s5_re_s[None], s5_im_s[None])
```

```python
import functools
import math

import jax
import jax.numpy as jnp
from jax import lax
from jax.experimental import pallas as pl
from jax.experimental.pallas import tpu as pltpu

F32 = jnp.float32
BF16 = jnp.bfloat16

D_MODEL = 1024
FF_DIM = 2816
N_SUBLAYERS = 3
FFN_RES = 0.5
RMS_EPS = 1e-6
CHUNK = 64
PAST_LEN = 2048
S5_WIDTH = 512
S5_GROUP = 16
S5_GROUPS = 32
S5_STATE = 64
HEAD_DIM = 64
N_HEADS = 8
N_KV_HEADS = 2
GQA_REP = N_HEADS // N_KV_HEADS
WINDOW = 128
WIN_CHUNKS = WINDOW // CHUNK
ROPE_THETA = 500000.0
ROPE_DIM = HEAD_DIM // 4
ATTN_SCALE = HEAD_DIM ** -0.5
NEG_INF = -1e30
Q_WIDTH = N_HEADS * HEAD_DIM
KV_WIDTH = N_KV_HEADS * HEAD_DIM
QKV_END = S5_WIDTH + Q_WIDTH + 2 * KV_WIDTH
IN_WIDTH = QKV_END + 2 * D_MODEL

V7X_LANES = 128
V7X_SUBLANES = 8
V7X_VMEM_BYTES = 64 * 1024 * 1024

TILE = 512
SEG = TILE // V7X_SUBLANES
FF_COLS = 256
S5_BLOCKS = 4
S5_BLOCK_GROUPS = S5_GROUPS // S5_BLOCKS
S5_BLOCK_STATES = S5_BLOCK_GROUPS * S5_STATE
S5_BLOCK_IN = S5_BLOCK_GROUPS * S5_GROUP
STATE_COLS = S5_BLOCKS * 2 * S5_BLOCK_STATES
KV_SPAN = (WIN_CHUNKS + 1) * CHUNK


def _dot(a, b):
    return jnp.dot(a, b, preferred_element_type=F32)


def _sigmoid(x):
    return 0.5 * jnp.tanh(0.5 * x) + 0.5


def _silu(x):
    half = 0.5 * x
    return half * jnp.tanh(half) + half


def _rms(x):
    return x * lax.rsqrt(jnp.mean(x * x, axis=-1, keepdims=True) + RMS_EPS)


def _per_seq(v, rows):
    nb, d = v.shape
    if nb == 1:
        return v
    return jnp.broadcast_to(v[:, None, :], (nb, rows // nb, d)).reshape(rows, d)


def _modulated(x, norm_w, shift, scale):
    rows = x.shape[0]
    return (_rms(x) * norm_w) * (1.0 + _per_seq(scale, rows)) + _per_seq(shift, rows)


def _ada_kernel(c_ref, w_ref, b_ref, o_ref):
    c = c_ref[...]
    s = _silu(c).astype(BF16)
    o_ref[...] = _dot(s, w_ref[...].astype(BF16)) + b_ref[...]


def _ada_call(c_pad, w_ada, b_ada):
    rows = c_pad.shape[0]
    n = w_ada.shape[1]
    bn = 1024
    return pl.pallas_call(
        _ada_kernel,
        grid=(n // bn,),
        in_specs=[pl.BlockSpec((rows, D_MODEL), lambda i: (0, 0)),
                  pl.BlockSpec((D_MODEL, bn), lambda i: (0, i)),
                  pl.BlockSpec((1, bn), lambda i: (0, i))],
        out_specs=pl.BlockSpec((rows, bn), lambda i: (0, i)),
        out_shape=jax.ShapeDtypeStruct((rows, n), F32),
        compiler_params=pltpu.CompilerParams(dimension_semantics=("arbitrary",)),
        name="adaln",
    )(c_pad, w_ada, b_ada)


def _s5_prep_kernel(ldt_ref, are_ref, aim_ref, bre_ref, bim_ref, bbre_ref, bbim_ref, pre_ref, pim_ref):
    dt = jnp.exp(ldt_ref[...])
    lr = jnp.minimum(are_ref[...], -1e-4)
    li = aim_ref[...]
    mag = jnp.exp(lr * dt)
    ab_re = mag * jnp.cos(li * dt)
    ab_im = mag * jnp.sin(li * dt)
    nr, ni = ab_re - 1.0, ab_im
    den = lr * lr + li * li
    f_re = (nr * lr + ni * li) / den
    f_im = (ni * lr - nr * li) / den
    for g in range(S5_GROUPS):
        fr, fi = f_re[g:g + 1, :], f_im[g:g + 1, :]
        br, bi = bre_ref[g], bim_ref[g]
        bbre_ref[g] = fr * br - fi * bi
        bbim_ref[g] = fr * bi + fi * br
    pr, pi = ab_re, ab_im
    for j in range(SEG):
        pre_ref[j] = pr
        pim_ref[j] = pi
        pr, pi = pr * ab_re - pi * ab_im, pr * ab_im + pi * ab_re


def _s5_prep_call(log_dt, a_re, a_im, b_re_t, b_im_t):
    g, p, gc = S5_GROUPS, S5_STATE, S5_GROUP
    return pl.pallas_call(
        _s5_prep_kernel,
        out_shape=(jax.ShapeDtypeStruct((g, gc, p), F32), jax.ShapeDtypeStruct((g, gc, p), F32),
                   jax.ShapeDtypeStruct((SEG, g, p), F32), jax.ShapeDtypeStruct((SEG, g, p), F32)),
        name="s5_discretise",
    )(log_dt, a_re, a_im, b_re_t, b_im_t)


def _ffn_kernel(*refs, sub, final_norm):
    if final_norm:
        x_ref, mod_ref, nw_ref, w1_ref, w2_ref, nf_ref, o_ref, hid_ref = refs
    else:
        x_ref, mod_ref, nw_ref, w1_ref, w2_ref, o_ref, hid_ref = refs
    x = x_ref[...]
    rows = x.shape[0]
    shift, scale, gate = mod_ref[:, 3 * sub, :], mod_ref[:, 3 * sub + 1, :], mod_ref[:, 3 * sub + 2, :]
    u = _modulated(x, nw_ref[...], shift, scale).astype(BF16)
    for c in range(FF_DIM // FF_COLS):
        lo = c * FF_COLS
        a = _dot(u, w1_ref[:, lo:lo + FF_COLS])
        b = _dot(u, w1_ref[:, FF_DIM + lo:FF_DIM + lo + FF_COLS])
        hid_ref[:, lo:lo + FF_COLS] = (_silu(a) * b).astype(BF16)
    y = _dot(hid_ref[...], w2_ref[...])
    out = x + _per_seq(FFN_RES * gate, rows) * y
    if final_norm:
        out = _rms(out) * nf_ref[...]
    o_ref[...] = out


def _resident(shape):
    zeros = (0,) * len(shape)
    return pl.BlockSpec(shape, lambda i: zeros, pipeline_mode=pl.Buffered(1))


def _ffn_call(x2d, mod, norm_w, w1, w2, norm_final, *, nb, tiles_per_mod, sub, name):
    n = x2d.shape[0]
    final_norm = norm_final is not None
    in_specs = [pl.BlockSpec((TILE, D_MODEL), lambda i: (i, 0)),
                pl.BlockSpec((nb, 3 * N_SUBLAYERS, D_MODEL), lambda i: (i // tiles_per_mod, 0, 0)),
                _resident((1, D_MODEL)),
                _resident((D_MODEL, 2 * FF_DIM)),
                _resident((FF_DIM, D_MODEL))]
    args = [x2d, mod, norm_w, w1, w2]
    if final_norm:
        in_specs.append(_resident((1, D_MODEL)))
        args.append(norm_final)
    return pl.pallas_call(
        functools.partial(_ffn_kernel, sub=sub, final_norm=final_norm),
        grid=(n // TILE,),
        in_specs=in_specs,
        out_specs=pl.BlockSpec((TILE, D_MODEL), lambda i: (i, 0)),
        out_shape=jax.ShapeDtypeStruct((n, D_MODEL), F32),
        scratch_shapes=[pltpu.VMEM((TILE, FF_DIM), BF16)],
        compiler_params=pltpu.CompilerParams(dimension_semantics=("arbitrary",),
                                             vmem_limit_bytes=48 * 1024 * 1024),
        name=name,
    )(*args)


def _mixer_kernel(*refs, prompt, tiles_per_seq):
    (h_ref, mod_ref, nmix_ref, win_ref, qn_ref, kn_ref, ones_ref, invf_ref, m1_ref, m2_ref, esel_ref,
     sinks_ref, bbd_ref, cbd_ref, apow_ref, dvec_ref, wglu_ref, bglu_ref, ws5_ref, wattn_ref,
     wout_ref) = refs[:21]
    if prompt:
        (h2_ref, kout_ref, vout_ref, stout_ref,
         us_ref, perm_ref, bu_ref, yperm_ref, qs_ref, kt_ref, vt_ref, os_ref,
         carry_ref, kcar_ref, vcar_ref, rope_ref) = refs[21:]
    else:
        (ck_ref, cv_ref, stin_ref,
         h2_ref, kout_ref, vout_ref, stout_ref,
         us_ref, perm_ref, bu_ref, yperm_ref, qs_ref, kt_ref, vt_ref, os_ref,
         ktc_ref, vtc_ref) = refs[21:]

    i = pl.program_id(0)
    tile_in_seq = i % tiles_per_seq
    rows = h_ref.shape[0]
    n_chunks = rows // CHUNK
    h = h_ref[...]
    shift, scale, gate = mod_ref[:, 3, :], mod_ref[:, 4, :], mod_ref[:, 5, :]
    u = _modulated(h, nmix_ref[...], shift, scale).astype(BF16)

    s5_u = _dot(u, win_ref[:, 0:S5_WIDTH])
    for b in range(S5_BLOCKS):
        us_ref[b] = s5_u[:, b * S5_BLOCK_IN:(b + 1) * S5_BLOCK_IN]
    q_raw = _dot(u, win_ref[:, S5_WIDTH:S5_WIDTH + Q_WIDTH])
    kv_raw = _dot(u, win_ref[:, S5_WIDTH + Q_WIDTH:QKV_END])
    k_raw, v_new = kv_raw[:, :KV_WIDTH], kv_raw[:, KV_WIDTH:]

    def head_rms(x, w_row, ones):
        sq = x * x
        hi = sq.astype(BF16)
        lo = (sq - hi.astype(F32)).astype(BF16)
        ms = (_dot(hi, ones) + _dot(lo, ones)) * (1.0 / HEAD_DIM)
        return x * lax.rsqrt(ms + RMS_EPS) * w_row

    row_id = lax.broadcasted_iota(jnp.int32, (rows, V7X_LANES), 0)
    if prompt:
        @pl.when(i == 0)
        def _():
            ang_r = row_id.astype(F32) * invf_ref[...]
            rope_ref[0] = jnp.cos(ang_r)
            rope_ref[1] = jnp.sin(ang_r)
        start = jnp.broadcast_to((tile_in_seq * rows).astype(F32), (V7X_SUBLANES, V7X_LANES))
        ang_0 = start * invf_ref[...]
        c_0, s_0 = jnp.cos(ang_0)[0:1, :], jnp.sin(ang_0)[0:1, :]
        cs = c_0 * rope_ref[0] - s_0 * rope_ref[1]
        sn = s_0 * rope_ref[0] + c_0 * rope_ref[1]
    else:
        ang = (PAST_LEN + (row_id & (CHUNK - 1))).astype(F32) * invf_ref[...]
        cs, sn = jnp.cos(ang), jnp.sin(ang)
    s_up, s_dn = sn * m1_ref[...], sn * m2_ref[...]

    def rope(xt):
        return (xt * cs + pltpu.roll(xt, V7X_LANES - ROPE_DIM // 2, 1) * s_up
                + pltpu.roll(xt, ROPE_DIM // 2, 1) * s_dn)

    qn = head_rms(q_raw, qn_ref[...], ones_ref[...])
    kn = head_rms(k_raw, kn_ref[...], ones_ref[0:KV_WIDTH, 0:KV_WIDTH])
    q = jnp.concatenate([rope(qn[:, t * V7X_LANES:(t + 1) * V7X_LANES])
                         for t in range(Q_WIDTH // V7X_LANES)], axis=1)
    k_new = rope(kn)
    qs_ref[...] = (q * ATTN_SCALE).astype(BF16)
    kout_ref[...] = k_new
    vout_ref[...] = v_new

    head_of_lane = lax.broadcasted_iota(jnp.int32, (1, GQA_REP * HEAD_DIM), 1) // HEAD_DIM
    if prompt:
        @pl.when(tile_in_seq == 0)
        def _():
            kcar_ref[...] = jnp.zeros_like(kcar_ref)
            vcar_ref[...] = jnp.zeros_like(vcar_ref)
            carry_ref[...] = jnp.zeros_like(carry_ref)
        k_ext = jnp.concatenate([kcar_ref[...], k_new], axis=0).astype(BF16)
        v_ext = jnp.concatenate([vcar_ref[...], v_new], axis=0).astype(BF16)
        kcar_ref[...] = k_new[rows - WINDOW:, :]
        vcar_ref[...] = v_new[rows - WINDOW:, :]
        key_chunk = lax.broadcasted_iota(jnp.int32, (1, KV_SPAN), 1) // CHUNK
    else:
        k_ext, v_ext = k_new.astype(BF16), v_new.astype(BF16)
        k_cache, v_cache = ck_ref[...].astype(BF16), cv_ref[...].astype(BF16)

    for j in range(N_KV_HEADS):
        sel = esel_ref[j]
        n_ext = k_ext.shape[0]
        kt_ref[0:n_ext, :] = _dot(k_ext, sel).astype(BF16)
        vt_ref[0:n_ext, :] = _dot(v_ext, sel).astype(BF16)
        if not prompt:
            ktc_ref[...] = _dot(k_cache, sel).astype(BF16)
            vtc_ref[...] = _dot(v_cache, sel).astype(BF16)
        sink_col = jnp.concatenate(
            [jnp.full((CHUNK, 1), sinks_ref[GQA_REP * j + r], F32) for r in range(GQA_REP)], axis=0)
        lo = j * GQA_REP * HEAD_DIM

        for c in range(n_chunks):
            r0 = c * CHUNK
            qc = qs_ref[pl.ds(r0, CHUNK), lo:lo + GQA_REP * HEAD_DIM]
            qm = jnp.concatenate([jnp.where(head_of_lane == r, qc, jnp.zeros_like(qc))
                                  for r in range(GQA_REP)], axis=0)
            if prompt:
                keys = kt_ref[pl.ds(r0, KV_SPAN), :]
                vals = vt_ref[pl.ds(r0, KV_SPAN), :]
            else:
                c0 = c * WINDOW
                keys = jnp.concatenate([ktc_ref[pl.ds(c0, WINDOW), :], kt_ref[pl.ds(r0, CHUNK), :]], axis=0)
                vals = jnp.concatenate([vtc_ref[pl.ds(c0, WINDOW), :], vt_ref[pl.ds(r0, CHUNK), :]], axis=0)
            s = lax.dot_general(qm, keys, (((1,), (1,)), ((), ())), preferred_element_type=F32)
            if prompt and c < WIN_CHUNKS:
                first_chunk = tile_in_seq * n_chunks + c - WIN_CHUNKS
                s = jnp.where(first_chunk + key_chunk >= 0, s, NEG_INF)
            m = jnp.maximum(jnp.max(s, axis=1, keepdims=True), sink_col)
            e = jnp.exp(s - m)
            den = jnp.sum(e, axis=1, keepdims=True) + jnp.exp(sink_col - m)
            pv = _dot(e.astype(BF16), vals) * (1.0 / den)
            oc = jnp.zeros((CHUNK, GQA_REP * HEAD_DIM), F32)
            for r in range(GQA_REP):
                oc = oc + jnp.where(head_of_lane == r, pv[r * CHUNK:(r + 1) * CHUNK, :], 0.0)
            os_ref[pl.ds(r0, CHUNK), lo:lo + GQA_REP * HEAD_DIM] = oc.astype(BF16)

    ns = S5_BLOCK_STATES
    groups_per_seg = SEG // V7X_SUBLANES
    part_tiles = ns // V7X_LANES

    def pow_bcast(t0, j):
        r0 = V7X_SUBLANES * j
        return jnp.concatenate([apow_ref[t0 + k, r0:r0 + V7X_SUBLANES, :] for k in range(part_tiles)], axis=1)

    def pow_row(t0, j):
        r0 = V7X_SUBLANES * j
        return jnp.concatenate([apow_ref[t0 + k, r0:r0 + 1, :] for k in range(part_tiles)], axis=1)

    for b in range(S5_BLOCKS):
        c_re, c_im = 2 * ns * b, 2 * ns * b + ns
        for j in range(SEG):
            perm_ref[b, V7X_SUBLANES * j:V7X_SUBLANES * (j + 1), :] = (
                us_ref.at[b][pl.ds(j, V7X_SUBLANES, stride=SEG), :])
        bu_ref[...] = _dot(perm_ref[b].astype(BF16), bbd_ref[b]).reshape(SEG, V7X_SUBLANES, 2 * ns)
        t_re, t_im = c_re // V7X_LANES, c_im // V7X_LANES
        a_re, a_im = pow_bcast(t_re, 0), pow_bcast(t_im, 0)

        def scan_step(j, hc, a_re=a_re, a_im=a_im):
            hr, hi = hc
            nr = a_re * hr - a_im * hi + bu_ref[j, :, 0:ns]
            ni = a_re * hi + a_im * hr + bu_ref[j, :, ns:2 * ns]
            bu_ref[j, :, 0:ns] = nr
            bu_ref[j, :, ns:2 * ns] = ni
            return nr, ni

        zero = jnp.zeros((V7X_SUBLANES, ns), F32)
        end_re, end_im = lax.fori_loop(0, SEG, scan_step, (zero, zero), unroll=True)

        if prompt:
            p_re, p_im = pow_row(t_re, SEG - 1), pow_row(t_im, SEG - 1)
            s_re, s_im = carry_ref[0:1, c_re:c_re + ns], carry_ref[0:1, c_im:c_im + ns]
            in_re, in_im = [], []
            for s in range(V7X_SUBLANES):
                in_re.append(s_re)
                in_im.append(s_im)
                s_re, s_im = (p_re * s_re - p_im * s_im + end_re[s:s + 1, :],
                              p_re * s_im + p_im * s_re + end_im[s:s + 1, :])
            cin_re, cin_im = jnp.concatenate(in_re, axis=0), jnp.concatenate(in_im, axis=0)
            carry_ref[0:1, c_re:c_re + ns] = s_re
            carry_ref[0:1, c_im:c_im + ns] = s_im
            stout_ref[0, :, c_re:c_re + ns] = s_re
            stout_ref[0, :, c_im:c_im + ns] = s_im
        else:
            cin_re, cin_im = stin_ref[0, :, c_re:c_re + ns], stin_ref[0, :, c_im:c_im + ns]

        for j in range(SEG):
            pr, pi = pow_bcast(t_re, j), pow_bcast(t_im, j)
            bu_ref[j, :, 0:ns] = bu_ref[j, :, 0:ns] + (pr * cin_re - pi * cin_im)
            bu_ref[j, :, ns:2 * ns] = bu_ref[j, :, ns:2 * ns] + (pr * cin_im + pi * cin_re)
        if not prompt:
            stout_ref[0, :, c_re:c_re + ns] = bu_ref[SEG - 1, :, 0:ns]
            stout_ref[0, :, c_im:c_im + ns] = bu_ref[SEG - 1, :, ns:2 * ns]
        hb = bu_ref[...].reshape(rows, 2 * ns).astype(BF16)
        yperm_ref[b] = _dot(hb, cbd_ref[b])
        d_row = dvec_ref[:, b * S5_BLOCK_IN:(b + 1) * S5_BLOCK_IN]
        for g in range(rows // V7X_SUBLANES):
            s, j0 = g // groups_per_seg, V7X_SUBLANES * (g % groups_per_seg)
            r0 = V7X_SUBLANES * g
            perm_ref[b, r0:r0 + V7X_SUBLANES, :] = (
                yperm_ref.at[b][pl.ds(V7X_SUBLANES * j0 + s, V7X_SUBLANES, stride=V7X_SUBLANES), :]
                + d_row * us_ref[b, r0:r0 + V7X_SUBLANES, :])
    y = jax.nn.gelu(jnp.concatenate([perm_ref[b] for b in range(S5_BLOCKS)], axis=1))
    y = y * _sigmoid(_dot(y.astype(BF16), wglu_ref[...]) + bglu_ref[...])
    y_s5 = _dot(y.astype(BF16), ws5_ref[...])

    y_attn = _dot(os_ref[...], wattn_ref[...])
    g_s5 = _sigmoid(_dot(u, win_ref[:, QKV_END:QKV_END + D_MODEL]))
    g_attn = _sigmoid(_dot(u, win_ref[:, QKV_END + D_MODEL:IN_WIDTH]))
    merged = (g_s5 * y_s5 + g_attn * y_attn).astype(BF16)
    h2_ref[...] = h + _per_seq(gate, rows) * _dot(merged, wout_ref[...])


def _mixer_call(h2d, mod, consts, *, prompt, nb, tiles_per_seq, cache=None):
    n = h2d.shape[0]
    n_seq_blocks = n // (TILE * tiles_per_seq)
    state_rows = 1 if prompt else V7X_SUBLANES
    smem = pl.BlockSpec(memory_space=pltpu.SMEM)
    in_specs = [pl.BlockSpec((TILE, D_MODEL), lambda i: (i, 0)),
                pl.BlockSpec((nb, 3 * N_SUBLAYERS, D_MODEL), lambda i: (i // tiles_per_seq, 0, 0))]
    args = [h2d, mod]
    for name in ("nmix", "win", "qn", "kn", "ones", "invf", "m1", "m2", "esel"):
        in_specs.append(_resident(consts[name].shape))
        args.append(consts[name])
    in_specs.append(smem)
    args.append(consts["sinks"])
    for name in ("bbd", "cbd", "apow", "dvec", "wglu", "bglu", "ws5", "wattn", "wout"):
        in_specs.append(_resident(consts[name].shape))
        args.append(consts[name])
    if not prompt:
        ck, cv, st_in = cache
        in_specs += [_resident(ck.shape), _resident(cv.shape), _resident(st_in.shape)]
        args += [ck, cv, st_in]
    out_specs = (pl.BlockSpec((TILE, D_MODEL), lambda i: (i, 0)),
                 pl.BlockSpec((TILE, KV_WIDTH), lambda i: (i, 0)),
                 pl.BlockSpec((TILE, KV_WIDTH), lambda i: (i, 0)),
                 pl.BlockSpec((1, state_rows, STATE_COLS), lambda i: (i // tiles_per_seq, 0, 0)))
    out_shape = (jax.ShapeDtypeStruct((n, D_MODEL), F32),
                 jax.ShapeDtypeStruct((n, KV_WIDTH), F32),
                 jax.ShapeDtypeStruct((n, KV_WIDTH), F32),
                 jax.ShapeDtypeStruct((n_seq_blocks, state_rows, STATE_COLS), F32))
    s5_tiles = (S5_BLOCKS, TILE, S5_BLOCK_IN)
    scratch = [pltpu.VMEM(s5_tiles, F32),
               pltpu.VMEM(s5_tiles, F32),
               pltpu.VMEM((SEG, V7X_SUBLANES, 2 * S5_BLOCK_STATES), F32),
               pltpu.VMEM(s5_tiles, F32),
               pltpu.VMEM((TILE, Q_WIDTH), BF16),
               pltpu.VMEM((TILE + WINDOW, GQA_REP * HEAD_DIM), BF16),
               pltpu.VMEM((TILE + WINDOW, GQA_REP * HEAD_DIM), BF16),
               pltpu.VMEM((TILE, Q_WIDTH), BF16)]
    if prompt:
        scratch += [pltpu.VMEM((1, STATE_COLS), F32),
                    pltpu.VMEM((WINDOW, KV_WIDTH), F32),
                    pltpu.VMEM((WINDOW, KV_WIDTH), F32),
                    pltpu.VMEM((2, TILE, V7X_LANES), F32)]
    else:
        scratch += [pltpu.VMEM((V7X_SUBLANES * WINDOW, GQA_REP * HEAD_DIM), BF16),
                    pltpu.VMEM((V7X_SUBLANES * WINDOW, GQA_REP * HEAD_DIM), BF16)]
    return pl.pallas_call(
        functools.partial(_mixer_kernel, prompt=prompt, tiles_per_seq=tiles_per_seq),
        grid=(n // TILE,),
        in_specs=in_specs,
        out_specs=out_specs,
        out_shape=out_shape,
        scratch_shapes=scratch,
        compiler_params=pltpu.CompilerParams(dimension_semantics=("arbitrary",),
                                             vmem_limit_bytes=56 * 1024 * 1024),
        name="mixer_prompt" if prompt else "mixer_sample",
    )(*args)


def _block_layout(re, im):
    lead = re.shape[:-2]
    re = re.reshape(lead + (S5_BLOCKS, S5_BLOCK_STATES))
    im = im.reshape(lead + (S5_BLOCKS, S5_BLOCK_STATES))
    return jnp.stack([re, im], axis=-2).reshape(lead + (STATE_COLS,))


def _from_block_layout(x):
    lead = x.shape[:-1]
    x = x.reshape(lead + (S5_BLOCKS, 2, S5_BLOCK_STATES))
    shape = lead + (S5_GROUPS, S5_STATE)
    return x[..., 0, :].reshape(shape), x[..., 1, :].reshape(shape)


def _mixer_constants(l, w_in, s5_log_dt, s5_a_re, s5_a_im, s5_b_re, s5_b_im, s5_c_re, s5_c_im, s5_d,
                     w_glu, b_glu, w_s5_proj, q_norm, k_norm, attn_sinks, w_attn_proj, w_out, norm_mix):
    bb_re, bb_im, pow_re, pow_im = _s5_prep_call(
        s5_log_dt[l][:, None], s5_a_re[l], s5_a_im[l],
        jnp.swapaxes(s5_b_re[l], 1, 2), jnp.swapaxes(s5_b_im[l], 1, 2))
    eye = jnp.eye(S5_BLOCK_GROUPS, dtype=F32)

    def in_blocks(x):
        x = x.reshape(S5_BLOCKS, S5_BLOCK_GROUPS, S5_GROUP, S5_STATE)
        return jnp.einsum("bgcp,gh->bgchp", x, eye).reshape(S5_BLOCKS, S5_BLOCK_IN, S5_BLOCK_STATES)

    def out_blocks(x):
        x = x.reshape(S5_BLOCKS, S5_BLOCK_GROUPS, S5_GROUP, S5_STATE)
        return jnp.einsum("bgcp,gh->bgphc", x, eye).reshape(S5_BLOCKS, S5_BLOCK_STATES, S5_BLOCK_IN)

    lane = jnp.arange(V7X_LANES)
    d = lane % HEAD_DIM
    half = ROPE_DIM // 2
    inv_freq = ROPE_THETA ** (-jnp.arange(half, dtype=F32) / half)
    head_rep = jnp.tile(jnp.eye(HEAD_DIM, dtype=F32), (1, GQA_REP))
    esel = jnp.stack([jnp.zeros((KV_WIDTH, GQA_REP * HEAD_DIM), F32).at[j * HEAD_DIM:(j + 1) * HEAD_DIM].set(head_rep)
                      for j in range(N_KV_HEADS)])
    return {
        "nmix": norm_mix[l][None],
        "win": w_in[l].astype(BF16),
        "qn": jnp.tile(q_norm[l], N_HEADS)[None],
        "kn": jnp.tile(k_norm[l], N_KV_HEADS)[None],
        "ones": jnp.kron(jnp.eye(N_HEADS, dtype=F32), jnp.ones((HEAD_DIM, HEAD_DIM), F32)).astype(BF16),
        "invf": jnp.where(d < ROPE_DIM, inv_freq[d % half], 0.0)[None].astype(F32),
        "m1": jnp.where(d < half, -1.0, 0.0)[None].astype(F32),
        "m2": jnp.where((d >= half) & (d < ROPE_DIM), 1.0, 0.0)[None].astype(F32),
        "esel": esel.astype(BF16),
        "sinks": attn_sinks[l],
        "bbd": jnp.concatenate([in_blocks(bb_re), in_blocks(bb_im)], axis=-1).astype(BF16),
        "cbd": jnp.concatenate([out_blocks(s5_c_re[l]), -out_blocks(s5_c_im[l])], axis=1).astype(BF16),
        "apow": jnp.repeat(jnp.swapaxes(_block_layout(pow_re, pow_im).reshape(
            SEG, STATE_COLS // V7X_LANES, V7X_LANES), 0, 1), V7X_SUBLANES, axis=1),
        "dvec": s5_d[l].reshape(1, S5_WIDTH),
        "wglu": w_glu[l].astype(BF16),
        "bglu": b_glu[l][None],
        "ws5": w_s5_proj[l].astype(BF16),
        "wattn": w_attn_proj[l].astype(BF16),
        "wout": w_out[l].astype(BF16),
    }


def kernel(x_prompt, x_sample, cache_k, cache_v, state_s5_re, state_s5_im, c_prompt, c_sample, w_ada, b_ada, norm_ffn1, w_ffn1_in, w_ffn1_out, norm_mix, w_in, s5_log_dt, s5_a_re, s5_a_im, s5_b_re, s5_b_im, s5_c_re, s5_c_im, s5_d, w_glu, b_glu, w_s5_proj, q_norm, k_norm, attn_sinks, w_attn_proj, w_out, norm_ffn2, w_ffn2_in, w_ffn2_out, norm_final):
    depth = w_ada.shape[0]
    bp, lp, _ = x_prompt.shape
    bs, ls, _ = x_sample.shape
    assert depth == 1 and lp % TILE == 0 and bs * ls == TILE and ls == CHUNK and bs == V7X_SUBLANES
    tiles_per_seq = lp // TILE
    l = 0

    c_all = jnp.concatenate([c_prompt, c_sample], axis=0)
    pad = (-c_all.shape[0]) % (2 * V7X_SUBLANES)
    c_pad = jnp.pad(c_all, ((0, pad), (0, 0)))
    mod = _ada_call(c_pad, w_ada[l], b_ada[l][None]).reshape(c_pad.shape[0], 3 * N_SUBLAYERS, D_MODEL)
    mod_p, mod_s = mod[:bp], mod[bp:bp + bs]

    consts = _mixer_constants(l, w_in, s5_log_dt, s5_a_re, s5_a_im, s5_b_re, s5_b_im, s5_c_re, s5_c_im,
                              s5_d, w_glu, b_glu, w_s5_proj, q_norm, k_norm, attn_sinks, w_attn_proj,
                              w_out, norm_mix)
    w1a, w1b = w_ffn1_in[l].astype(BF16), w_ffn1_out[l].astype(BF16)
    w2a, w2b = w_ffn2_in[l].astype(BF16), w_ffn2_out[l].astype(BF16)
    n1, n2, nf = norm_ffn1[l][None], norm_ffn2[l][None], norm_final[l][None]

    xp = x_prompt.reshape(bp * lp, D_MODEL)
    xs = x_sample.reshape(bs * ls, D_MODEL)
    cache = (cache_k[l].reshape(bs * WINDOW, KV_WIDTH), cache_v[l].reshape(bs * WINDOW, KV_WIDTH),
             _block_layout(state_s5_re[l], state_s5_im[l])[None])

    hp = _ffn_call(xp, mod_p, n1, w1a, w1b, None, nb=1, tiles_per_mod=tiles_per_seq, sub=0, name="ffn1_prompt")
    hs = _ffn_call(xs, mod_s, n1, w1a, w1b, None, nb=bs, tiles_per_mod=1, sub=0, name="ffn1_sample")
    hp, kp, vp, stp = _mixer_call(hp, mod_p, consts, prompt=True, nb=1, tiles_per_seq=tiles_per_seq)
    hs, ks, vs, sts = _mixer_call(hs, mod_s, consts, prompt=False, nb=bs, tiles_per_seq=1, cache=cache)
    yp = _ffn_call(hp, mod_p, n2, w2a, w2b, nf, nb=1, tiles_per_mod=tiles_per_seq, sub=2, name="ffn2_prompt")
    ys = _ffn_call(hs, mod_s, n2, w2a, w2b, nf, nb=bs, tiles_per_mod=1, sub=2, name="ffn2_sample")

    kv_shape = (N_KV_HEADS, HEAD_DIM)
    k_win_p = kp.reshape((bp, lp) + kv_shape)[:, lp - WINDOW:][None]
    v_win_p = vp.reshape((bp, lp) + kv_shape)[:, lp - WINDOW:][None]
    s5_re_p, s5_im_p = _from_block_layout(stp[:, 0, :])
    k_win_s = jnp.concatenate([cache_k[l][:, ls:], ks.reshape((bs, ls) + kv_shape)], axis=1)[None]
    v_win_s = jnp.concatenate([cache_v[l][:, ls:], vs.reshape((bs, ls) + kv_shape)], axis=1)[None]
    s5_re_s, s5_im_s = _from_block_layout(sts[0])
    return (yp.reshape(bp, lp, D_MODEL), ys.reshape(bs, ls, D_MODEL), k_win_p, v_win_p,
            s5_re_p[None], s5_im_p[None], k_win_s, v_win_s, s5_re_s[None], s5_im_s[None])
```

```python
import functools
import math

import jax
import jax.numpy as jnp
import numpy as np
from jax import lax
from jax.experimental import pallas as pl
from jax.experimental.pallas import tpu as pltpu

F32 = jnp.float32
BF16 = jnp.bfloat16

D_MODEL = 1024
FF_DIM = 2816
N_SUBLAYERS = 3
FFN_RES = 0.5
RMS_EPS = 1e-6
CHUNK = 64
PAST_LEN = 2048
S5_WIDTH = 512
S5_GROUP = 16
S5_GROUPS = 32
S5_STATE = 64
HEAD_DIM = 64
N_HEADS = 8
N_KV_HEADS = 2
GQA_REP = N_HEADS // N_KV_HEADS
WINDOW = 128
WIN_CHUNKS = WINDOW // CHUNK
ROPE_THETA = 500000.0
ROPE_DIM = HEAD_DIM // 4
ATTN_SCALE = HEAD_DIM ** -0.5
NEG_INF = -1e30
Q_WIDTH = N_HEADS * HEAD_DIM
KV_WIDTH = N_KV_HEADS * HEAD_DIM
QKV_END = S5_WIDTH + Q_WIDTH + 2 * KV_WIDTH
IN_WIDTH = QKV_END + 2 * D_MODEL

V7X_LANES = 128
V7X_SUBLANES = 8
V7X_VMEM_BYTES = 64 * 1024 * 1024

TILE = 512
SEG = TILE // V7X_SUBLANES
FF_COLS = 256
S5_BLOCKS = 4
S5_BLOCK_GROUPS = S5_GROUPS // S5_BLOCKS
S5_BLOCK_STATES = S5_BLOCK_GROUPS * S5_STATE
S5_BLOCK_IN = S5_BLOCK_GROUPS * S5_GROUP
STATE_COLS = S5_BLOCKS * 2 * S5_BLOCK_STATES
KV_SPAN = (WIN_CHUNKS + 1) * CHUNK


def _dot(a, b):
    return jnp.dot(a, b, preferred_element_type=F32)


def _sigmoid(x):
    return 0.5 * jnp.tanh(0.5 * x) + 0.5


def _silu(x):
    half = 0.5 * x
    return half * jnp.tanh(half) + half


def _rms(x):
    return x * lax.rsqrt(jnp.mean(x * x, axis=-1, keepdims=True) + RMS_EPS)


def _per_seq(v, rows):
    nb, d = v.shape
    if nb == 1:
        return v
    return jnp.broadcast_to(v[:, None, :], (nb, rows // nb, d)).reshape(rows, d)


def _modulated(x, norm_w, shift, scale):
    rows = x.shape[0]
    return (_rms(x) * norm_w) * (1.0 + _per_seq(scale, rows)) + _per_seq(shift, rows)


def _ada_kernel(c_ref, w_ref, b_ref, o_ref):
    c = c_ref[...]
    s = _silu(c).astype(BF16)
    o_ref[...] = _dot(s, w_ref[...].astype(BF16)) + b_ref[...]


def _ada_call(c_pad, w_ada, b_ada):
    rows = c_pad.shape[0]
    n = w_ada.shape[1]
    bn = 1024
    return pl.pallas_call(
        _ada_kernel,
        grid=(n // bn,),
        in_specs=[pl.BlockSpec((rows, D_MODEL), lambda i: (0, 0)),
                  pl.BlockSpec((D_MODEL, bn), lambda i: (0, i)),
                  pl.BlockSpec((1, bn), lambda i: (0, i))],
        out_specs=pl.BlockSpec((rows, bn), lambda i: (0, i)),
        out_shape=jax.ShapeDtypeStruct((rows, n), F32),
        compiler_params=pltpu.CompilerParams(dimension_semantics=("arbitrary",)),
        name="adaln",
    )(c_pad, w_ada, b_ada)


def _s5_prep_kernel(ldt_ref, are_ref, aim_ref, bre_ref, bim_ref, bbre_ref, bbim_ref, pre_ref, pim_ref):
    dt = jnp.exp(ldt_ref[...])
    lr = jnp.minimum(are_ref[...], -1e-4)
    li = aim_ref[...]
    mag = jnp.exp(lr * dt)
    ab_re = mag * jnp.cos(li * dt)
    ab_im = mag * jnp.sin(li * dt)
    nr, ni = ab_re - 1.0, ab_im
    den = lr * lr + li * li
    f_re = (nr * lr + ni * li) / den
    f_im = (ni * lr - nr * li) / den
    for g in range(S5_GROUPS):
        fr, fi = f_re[g:g + 1, :], f_im[g:g + 1, :]
        br, bi = bre_ref[g], bim_ref[g]
        bbre_ref[g] = fr * br - fi * bi
        bbim_ref[g] = fr * bi + fi * br
    pr, pi = ab_re, ab_im
    for j in range(SEG):
        pre_ref[j] = pr
        pim_ref[j] = pi
        pr, pi = pr * ab_re - pi * ab_im, pr * ab_im + pi * ab_re


def _s5_prep_call(log_dt, a_re, a_im, b_re_t, b_im_t):
    g, p, gc = S5_GROUPS, S5_STATE, S5_GROUP
    return pl.pallas_call(
        _s5_prep_kernel,
        out_shape=(jax.ShapeDtypeStruct((g, gc, p), F32), jax.ShapeDtypeStruct((g, gc, p), F32),
                   jax.ShapeDtypeStruct((SEG, g, p), F32), jax.ShapeDtypeStruct((SEG, g, p), F32)),
        name="s5_discretise",
    )(log_dt, a_re, a_im, b_re_t, b_im_t)


def _ffn_kernel(*refs, sub, final_norm):
    if final_norm:
        x_ref, mod_ref, nw_ref, w1_ref, w2_ref, nf_ref, o_ref, hid_ref = refs
    else:
        x_ref, mod_ref, nw_ref, w1_ref, w2_ref, o_ref, hid_ref = refs
    x = x_ref[...]
    rows = x.shape[0]
    shift, scale, gate = mod_ref[:, 3 * sub, :], mod_ref[:, 3 * sub + 1, :], mod_ref[:, 3 * sub + 2, :]
    u = _modulated(x, nw_ref[...], shift, scale).astype(BF16)
    for c in range(FF_DIM // FF_COLS):
        lo = c * FF_COLS
        a = _dot(u, w1_ref[:, lo:lo + FF_COLS])
        b = _dot(u, w1_ref[:, FF_DIM + lo:FF_DIM + lo + FF_COLS])
        hid_ref[:, lo:lo + FF_COLS] = (_silu(a) * b).astype(BF16)
    y = _dot(hid_ref[...], w2_ref[...])
    out = x + _per_seq(FFN_RES * gate, rows) * y
    if final_norm:
        out = _rms(out) * nf_ref[...]
    o_ref[...] = out


def _resident(shape):
    zeros = (0,) * len(shape)
    return pl.BlockSpec(shape, lambda i: zeros, pipeline_mode=pl.Buffered(1))


def _ffn_call(x2d, mod, norm_w, w1, w2, norm_final, *, nb, tiles_per_mod, sub, name):
    n = x2d.shape[0]
    final_norm = norm_final is not None
    in_specs = [pl.BlockSpec((TILE, D_MODEL), lambda i: (i, 0)),
                pl.BlockSpec((nb, 3 * N_SUBLAYERS, D_MODEL), lambda i: (i // tiles_per_mod, 0, 0)),
                _resident((1, D_MODEL)),
                _resident((D_MODEL, 2 * FF_DIM)),
                _resident((FF_DIM, D_MODEL))]
    args = [x2d, mod, norm_w, w1, w2]
    if final_norm:
        in_specs.append(_resident((1, D_MODEL)))
        args.append(norm_final)
    return pl.pallas_call(
        functools.partial(_ffn_kernel, sub=sub, final_norm=final_norm),
        grid=(n // TILE,),
        in_specs=in_specs,
        out_specs=pl.BlockSpec((TILE, D_MODEL), lambda i: (i, 0)),
        out_shape=jax.ShapeDtypeStruct((n, D_MODEL), F32),
        scratch_shapes=[pltpu.VMEM((TILE, FF_DIM), BF16)],
        compiler_params=pltpu.CompilerParams(dimension_semantics=("arbitrary",),
                                             vmem_limit_bytes=48 * 1024 * 1024),
        name=name,
    )(*args)


def _mixer_kernel(*refs, prompt, tiles_per_seq):
    (h_ref, mod_ref, nmix_ref, win_ref, qn_ref, kn_ref, ones_ref, invf_ref, m1_ref, m2_ref, esel_ref,
     sinks_ref, bbd_ref, cbd_ref, apow_ref, dvec_ref, wglu_ref, bglu_ref, ws5_ref, wattn_ref,
     wout_ref) = refs[:21]
    if prompt:
        (h2_ref, kout_ref, vout_ref, stout_ref,
         us_ref, perm_ref, bu_ref, yperm_ref, qs_ref, kt_ref, vt_ref, os_ref,
         carry_ref, kcar_ref, vcar_ref, rope_ref) = refs[21:]
    else:
        (ck_ref, cv_ref, stin_ref,
         h2_ref, kout_ref, vout_ref, stout_ref,
         us_ref, perm_ref, bu_ref, yperm_ref, qs_ref, kt_ref, vt_ref, os_ref,
         ktc_ref, vtc_ref) = refs[21:]

    i = pl.program_id(0)
    tile_in_seq = i % tiles_per_seq
    rows = h_ref.shape[0]
    n_chunks = rows // CHUNK
    h = h_ref[...]
    shift, scale, gate = mod_ref[:, 3, :], mod_ref[:, 4, :], mod_ref[:, 5, :]
    u = _modulated(h, nmix_ref[...], shift, scale).astype(BF16)

    s5_u = _dot(u, win_ref[:, 0:S5_WIDTH])
    for b in range(S5_BLOCKS):
        us_ref[b] = s5_u[:, b * S5_BLOCK_IN:(b + 1) * S5_BLOCK_IN]
    q_raw = _dot(u, win_ref[:, S5_WIDTH:S5_WIDTH + Q_WIDTH])
    kv_raw = _dot(u, win_ref[:, S5_WIDTH + Q_WIDTH:QKV_END])
    k_raw, v_new = kv_raw[:, :KV_WIDTH], kv_raw[:, KV_WIDTH:]

    def head_rms(x, w_row, ones):
        ms = _dot((x * x).astype(BF16), ones) * (1.0 / HEAD_DIM)
        return x * lax.rsqrt(ms + RMS_EPS) * w_row

    row_id = lax.broadcasted_iota(jnp.int32, (rows, V7X_LANES), 0)
    if prompt:
        @pl.when(i == 0)
        def _():
            ang_r = row_id.astype(F32) * invf_ref[...]
            rope_ref[0] = jnp.cos(ang_r)
            rope_ref[1] = jnp.sin(ang_r)
        start = jnp.broadcast_to((tile_in_seq * rows).astype(F32), (V7X_SUBLANES, V7X_LANES))
        ang_0 = start * invf_ref[...]
        c_0, s_0 = jnp.cos(ang_0)[0:1, :], jnp.sin(ang_0)[0:1, :]
        cs = c_0 * rope_ref[0] - s_0 * rope_ref[1]
        sn = s_0 * rope_ref[0] + c_0 * rope_ref[1]
    else:
        ang = (PAST_LEN + (row_id & (CHUNK - 1))).astype(F32) * invf_ref[...]
        cs, sn = jnp.cos(ang), jnp.sin(ang)
    s_up, s_dn = sn * m1_ref[...], sn * m2_ref[...]

    def rope(xt):
        return (xt * cs + pltpu.roll(xt, V7X_LANES - ROPE_DIM // 2, 1) * s_up
                + pltpu.roll(xt, ROPE_DIM // 2, 1) * s_dn)

    qn = head_rms(q_raw, qn_ref[...], ones_ref[...])
    kn = head_rms(k_raw, kn_ref[...], ones_ref[0:KV_WIDTH, 0:KV_WIDTH])
    q = jnp.concatenate([rope(qn[:, t * V7X_LANES:(t + 1) * V7X_LANES])
                         for t in range(Q_WIDTH // V7X_LANES)], axis=1)
    k_new = rope(kn)
    qs_ref[...] = (q * ATTN_SCALE).astype(BF16)
    kout_ref[...] = k_new
    vout_ref[...] = v_new

    head_of_lane = lax.broadcasted_iota(jnp.int32, (1, GQA_REP * HEAD_DIM), 1) // HEAD_DIM
    if prompt:
        @pl.when(tile_in_seq == 0)
        def _():
            kcar_ref[...] = jnp.zeros_like(kcar_ref)
            vcar_ref[...] = jnp.zeros_like(vcar_ref)
            carry_ref[...] = jnp.zeros_like(carry_ref)
        k_ext = jnp.concatenate([kcar_ref[...], k_new], axis=0).astype(BF16)
        v_ext = jnp.concatenate([vcar_ref[...], v_new], axis=0).astype(BF16)
        kcar_ref[...] = k_new[rows - WINDOW:, :]
        vcar_ref[...] = v_new[rows - WINDOW:, :]
        key_chunk = lax.broadcasted_iota(jnp.int32, (1, KV_SPAN), 1) // CHUNK
    else:
        k_ext, v_ext = k_new.astype(BF16), v_new.astype(BF16)
        k_cache, v_cache = ck_ref[...].astype(BF16), cv_ref[...].astype(BF16)

    for j in range(N_KV_HEADS):
        sel = esel_ref[j]
        n_ext = k_ext.shape[0]
        kt_ref[0:n_ext, :] = _dot(k_ext, sel).astype(BF16)
        vt_ref[0:n_ext, :] = _dot(v_ext, sel).astype(BF16)
        if not prompt:
            ktc_ref[...] = _dot(k_cache, sel).astype(BF16)
            vtc_ref[...] = _dot(v_cache, sel).astype(BF16)
        sink_col = jnp.concatenate(
            [jnp.full((CHUNK, 1), sinks_ref[GQA_REP * j + r], F32) for r in range(GQA_REP)], axis=0)
        lo = j * GQA_REP * HEAD_DIM

        for c in range(n_chunks):
            r0 = c * CHUNK
            qc = qs_ref[pl.ds(r0, CHUNK), lo:lo + GQA_REP * HEAD_DIM]
            qm = jnp.concatenate([jnp.where(head_of_lane == r, qc, jnp.zeros_like(qc))
                                  for r in range(GQA_REP)], axis=0)
            if prompt:
                keys = kt_ref[pl.ds(r0, KV_SPAN), :]
                vals = vt_ref[pl.ds(r0, KV_SPAN), :]
            else:
                c0 = c * WINDOW
                keys = jnp.concatenate([ktc_ref[pl.ds(c0, WINDOW), :], kt_ref[pl.ds(r0, CHUNK), :]], axis=0)
                vals = jnp.concatenate([vtc_ref[pl.ds(c0, WINDOW), :], vt_ref[pl.ds(r0, CHUNK), :]], axis=0)
            s = lax.dot_general(qm, keys, (((1,), (1,)), ((), ())), preferred_element_type=F32)
            if prompt and c < WIN_CHUNKS:
                first_chunk = tile_in_seq * n_chunks + c - WIN_CHUNKS
                s = jnp.where(first_chunk + key_chunk >= 0, s, NEG_INF)
            m = jnp.maximum(jnp.max(s, axis=1, keepdims=True), sink_col)
            e = jnp.exp(s - m)
            den = jnp.sum(e, axis=1, keepdims=True) + jnp.exp(sink_col - m)
            pv = _dot(e.astype(BF16), vals) * (1.0 / den)
            oc = jnp.zeros((CHUNK, GQA_REP * HEAD_DIM), F32)
            for r in range(GQA_REP):
                oc = oc + jnp.where(head_of_lane == r, pv[r * CHUNK:(r + 1) * CHUNK, :], 0.0)
            os_ref[pl.ds(r0, CHUNK), lo:lo + GQA_REP * HEAD_DIM] = oc.astype(BF16)

    ns = S5_BLOCK_STATES
    groups_per_seg = SEG // V7X_SUBLANES
    part_tiles = ns // V7X_LANES

    mix_cols = D_MODEL // S5_BLOCKS
    gates = [None] * (2 * S5_BLOCKS)
    y_attn = [None] * S5_BLOCKS

    def gate_chunk(k):
        lo = QKV_END + k * mix_cols
        gates[k] = _sigmoid(_dot(u, win_ref[:, lo:lo + mix_cols]))

    def attn_chunk(k):
        y_attn[k] = _dot(os_ref[...], wattn_ref[:, k * mix_cols:(k + 1) * mix_cols])

    def pow_bcast(t0, j):
        r0 = V7X_SUBLANES * j
        return jnp.concatenate([apow_ref[t0 + k, r0:r0 + V7X_SUBLANES, :] for k in range(part_tiles)], axis=1)

    def pow_row(t0, j):
        r0 = V7X_SUBLANES * j
        return jnp.concatenate([apow_ref[t0 + k, r0:r0 + 1, :] for k in range(part_tiles)], axis=1)

    for b in range(S5_BLOCKS):
        c_re, c_im = 2 * ns * b, 2 * ns * b + ns
        for j in range(SEG):
            perm_ref[b, V7X_SUBLANES * j:V7X_SUBLANES * (j + 1), :] = (
                us_ref.at[b][pl.ds(j, V7X_SUBLANES, stride=SEG), :])
        bu_ref[...] = _dot(perm_ref[b].astype(BF16), bbd_ref[b]).reshape(SEG, V7X_SUBLANES, 2 * ns)
        gate_chunk(2 * b)
        t_re, t_im = c_re // V7X_LANES, c_im // V7X_LANES
        a_re, a_im = pow_bcast(t_re, 0), pow_bcast(t_im, 0)

        def scan_step(j, hc, a_re=a_re, a_im=a_im):
            hr, hi = hc
            nr = a_re * hr - a_im * hi + bu_ref[j, :, 0:ns]
            ni = a_re * hi + a_im * hr + bu_ref[j, :, ns:2 * ns]
            bu_ref[j, :, 0:ns] = nr
            bu_ref[j, :, ns:2 * ns] = ni
            return nr, ni

        zero = jnp.zeros((V7X_SUBLANES, ns), F32)
        end_re, end_im = lax.fori_loop(0, SEG, scan_step, (zero, zero), unroll=True)

        if prompt:
            p_re, p_im = pow_row(t_re, SEG - 1), pow_row(t_im, SEG - 1)
            s_re, s_im = carry_ref[0:1, c_re:c_re + ns], carry_ref[0:1, c_im:c_im + ns]
            in_re, in_im = [], []
            for s in range(V7X_SUBLANES):
                in_re.append(s_re)
                in_im.append(s_im)
                s_re, s_im = (p_re * s_re - p_im * s_im + end_re[s:s + 1, :],
                              p_re * s_im + p_im * s_re + end_im[s:s + 1, :])
            cin_re, cin_im = jnp.concatenate(in_re, axis=0), jnp.concatenate(in_im, axis=0)
            carry_ref[0:1, c_re:c_re + ns] = s_re
            carry_ref[0:1, c_im:c_im + ns] = s_im
            stout_ref[0, :, c_re:c_re + ns] = s_re
            stout_ref[0, :, c_im:c_im + ns] = s_im
        else:
            cin_re, cin_im = stin_ref[0, :, c_re:c_re + ns], stin_ref[0, :, c_im:c_im + ns]

        gate_chunk(2 * b + 1)
        for j in range(SEG):
            pr, pi = pow_bcast(t_re, j), pow_bcast(t_im, j)
            bu_ref[j, :, 0:ns] = bu_ref[j, :, 0:ns] + (pr * cin_re - pi * cin_im)
            bu_ref[j, :, ns:2 * ns] = bu_ref[j, :, ns:2 * ns] + (pr * cin_im + pi * cin_re)
        attn_chunk(b)
        if not prompt:
            stout_ref[0, :, c_re:c_re + ns] = bu_ref[SEG - 1, :, 0:ns]
            stout_ref[0, :, c_im:c_im + ns] = bu_ref[SEG - 1, :, ns:2 * ns]
        hb = bu_ref[...].reshape(rows, 2 * ns).astype(BF16)
        yperm_ref[b] = _dot(hb, cbd_ref[b])
        d_row = dvec_ref[:, b * S5_BLOCK_IN:(b + 1) * S5_BLOCK_IN]
        for g in range(rows // V7X_SUBLANES):
            s, j0 = g // groups_per_seg, V7X_SUBLANES * (g % groups_per_seg)
            r0 = V7X_SUBLANES * g
            perm_ref[b, r0:r0 + V7X_SUBLANES, :] = (
                yperm_ref.at[b][pl.ds(V7X_SUBLANES * j0 + s, V7X_SUBLANES, stride=V7X_SUBLANES), :]
                + d_row * us_ref[b, r0:r0 + V7X_SUBLANES, :])
    y = jax.nn.gelu(jnp.concatenate([perm_ref[b] for b in range(S5_BLOCKS)], axis=1))
    y = y * _sigmoid(_dot(y.astype(BF16), wglu_ref[...]) + bglu_ref[...])
    y_s5 = _dot(y.astype(BF16), ws5_ref[...])

    merged = jnp.concatenate(
        [gates[k] * y_s5[:, k * mix_cols:(k + 1) * mix_cols] + gates[S5_BLOCKS + k] * y_attn[k]
         for k in range(S5_BLOCKS)], axis=1).astype(BF16)
    h2_ref[...] = h + _per_seq(gate, rows) * _dot(merged, wout_ref[...])


def _mixer_call(h2d, mod, consts, *, prompt, nb, tiles_per_seq, cache=None):
    n = h2d.shape[0]
    n_seq_blocks = n // (TILE * tiles_per_seq)
    state_rows = 1 if prompt else V7X_SUBLANES
    smem = pl.BlockSpec(memory_space=pltpu.SMEM)
    in_specs = [pl.BlockSpec((TILE, D_MODEL), lambda i: (i, 0)),
                pl.BlockSpec((nb, 3 * N_SUBLAYERS, D_MODEL), lambda i: (i // tiles_per_seq, 0, 0))]
    args = [h2d, mod]
    for name in ("nmix", "win", "qn", "kn", "ones", "invf", "m1", "m2", "esel"):
        in_specs.append(_resident(consts[name].shape))
        args.append(consts[name])
    in_specs.append(smem)
    args.append(consts["sinks"])
    for name in ("bbd", "cbd", "apow", "dvec", "wglu", "bglu", "ws5", "wattn", "wout"):
        in_specs.append(_resident(consts[name].shape))
        args.append(consts[name])
    if not prompt:
        ck, cv, st_in = cache
        in_specs += [_resident(ck.shape), _resident(cv.shape), _resident(st_in.shape)]
        args += [ck, cv, st_in]
    out_specs = (pl.BlockSpec((TILE, D_MODEL), lambda i: (i, 0)),
                 pl.BlockSpec((TILE, KV_WIDTH), lambda i: (i, 0)),
                 pl.BlockSpec((TILE, KV_WIDTH), lambda i: (i, 0)),
                 pl.BlockSpec((1, state_rows, STATE_COLS), lambda i: (i // tiles_per_seq, 0, 0)))
    out_shape = (jax.ShapeDtypeStruct((n, D_MODEL), F32),
                 jax.ShapeDtypeStruct((n, KV_WIDTH), F32),
                 jax.ShapeDtypeStruct((n, KV_WIDTH), F32),
                 jax.ShapeDtypeStruct((n_seq_blocks, state_rows, STATE_COLS), F32))
    s5_tiles = (S5_BLOCKS, TILE, S5_BLOCK_IN)
    scratch = [pltpu.VMEM(s5_tiles, F32),
               pltpu.VMEM(s5_tiles, F32),
               pltpu.VMEM((SEG, V7X_SUBLANES, 2 * S5_BLOCK_STATES), F32),
               pltpu.VMEM(s5_tiles, F32),
               pltpu.VMEM((TILE, Q_WIDTH), BF16),
               pltpu.VMEM((TILE + WINDOW, GQA_REP * HEAD_DIM), BF16),
               pltpu.VMEM((TILE + WINDOW, GQA_REP * HEAD_DIM), BF16),
               pltpu.VMEM((TILE, Q_WIDTH), BF16)]
    if prompt:
        scratch += [pltpu.VMEM((1, STATE_COLS), F32),
                    pltpu.VMEM((WINDOW, KV_WIDTH), F32),
                    pltpu.VMEM((WINDOW, KV_WIDTH), F32),
                    pltpu.VMEM((2, TILE, V7X_LANES), F32)]
    else:
        scratch += [pltpu.VMEM((V7X_SUBLANES * WINDOW, GQA_REP * HEAD_DIM), BF16),
                    pltpu.VMEM((V7X_SUBLANES * WINDOW, GQA_REP * HEAD_DIM), BF16)]
    return pl.pallas_call(
        functools.partial(_mixer_kernel, prompt=prompt, tiles_per_seq=tiles_per_seq),
        grid=(n // TILE,),
        in_specs=in_specs,
        out_specs=out_specs,
        out_shape=out_shape,
        scratch_shapes=scratch,
        compiler_params=pltpu.CompilerParams(dimension_semantics=("arbitrary",),
                                             vmem_limit_bytes=60 * 1024 * 1024),
        name="mixer_prompt" if prompt else "mixer_sample",
    )(*args)


def _block_layout(re, im):
    lead = re.shape[:-2]
    re = re.reshape(lead + (S5_BLOCKS, S5_BLOCK_STATES))
    im = im.reshape(lead + (S5_BLOCKS, S5_BLOCK_STATES))
    return jnp.stack([re, im], axis=-2).reshape(lead + (STATE_COLS,))


def _from_block_layout(x):
    lead = x.shape[:-1]
    x = x.reshape(lead + (S5_BLOCKS, 2, S5_BLOCK_STATES))
    shape = lead + (S5_GROUPS, S5_STATE)
    return x[..., 0, :].reshape(shape), x[..., 1, :].reshape(shape)


def _mixer_constants(l, w_in, s5_log_dt, s5_a_re, s5_a_im, s5_b_re, s5_b_im, s5_c_re, s5_c_im, s5_d,
                     w_glu, b_glu, w_s5_proj, q_norm, k_norm, attn_sinks, w_attn_proj, w_out, norm_mix):
    bb_re, bb_im, pow_re, pow_im = _s5_prep_call(
        s5_log_dt[l][:, None], s5_a_re[l], s5_a_im[l],
        jnp.swapaxes(s5_b_re[l], 1, 2), jnp.swapaxes(s5_b_im[l], 1, 2))
    eye = np.eye(S5_BLOCK_GROUPS, dtype=np.float32)

    def in_blocks(x):
        x = x.reshape(S5_BLOCKS, S5_BLOCK_GROUPS, S5_GROUP, S5_STATE)
        return jnp.einsum("bgcp,gh->bgchp", x, eye).reshape(S5_BLOCKS, S5_BLOCK_IN, S5_BLOCK_STATES)

    def out_blocks(x):
        x = x.reshape(S5_BLOCKS, S5_BLOCK_GROUPS, S5_GROUP, S5_STATE)
        return jnp.einsum("bgcp,gh->bgphc", x, eye).reshape(S5_BLOCKS, S5_BLOCK_STATES, S5_BLOCK_IN)

    d = np.arange(V7X_LANES) % HEAD_DIM
    half = ROPE_DIM // 2
    inv_freq = ROPE_THETA ** (-jnp.arange(half, dtype=F32) / half)
    freq_of_lane = np.zeros((half, V7X_LANES), np.float32)
    freq_of_lane[d[d < ROPE_DIM] % half, np.nonzero(d < ROPE_DIM)[0]] = 1.0
    esel = np.zeros((N_KV_HEADS, KV_WIDTH, GQA_REP * HEAD_DIM), np.float32)
    for j in range(N_KV_HEADS):
        esel[j, j * HEAD_DIM:(j + 1) * HEAD_DIM] = np.tile(np.eye(HEAD_DIM, dtype=np.float32), (1, GQA_REP))
    ones_bd = np.kron(np.eye(N_HEADS, dtype=np.float32), np.ones((HEAD_DIM, HEAD_DIM), np.float32))
    return {
        "nmix": norm_mix[l][None],
        "win": w_in[l].astype(BF16),
        "qn": jnp.tile(q_norm[l], N_HEADS)[None],
        "kn": jnp.tile(k_norm[l], N_KV_HEADS)[None],
        "ones": jnp.asarray(ones_bd, BF16),
        "invf": jnp.sum(inv_freq[:, None] * freq_of_lane, axis=0, keepdims=True),
        "m1": jnp.asarray(np.where(d < half, -1.0, 0.0)[None], F32),
        "m2": jnp.asarray(np.where((d >= half) & (d < ROPE_DIM), 1.0, 0.0)[None], F32),
        "esel": jnp.asarray(esel, BF16),
        "sinks": attn_sinks[l],
        "bbd": jnp.concatenate([in_blocks(bb_re), in_blocks(bb_im)], axis=-1).astype(BF16),
        "cbd": jnp.concatenate([out_blocks(s5_c_re[l]), -out_blocks(s5_c_im[l])], axis=1).astype(BF16),
        "apow": jnp.repeat(jnp.swapaxes(_block_layout(pow_re, pow_im).reshape(
            SEG, STATE_COLS // V7X_LANES, V7X_LANES), 0, 1), V7X_SUBLANES, axis=1),
        "dvec": s5_d[l].reshape(1, S5_WIDTH),
        "wglu": w_glu[l].astype(BF16),
        "bglu": b_glu[l][None],
        "ws5": w_s5_proj[l].astype(BF16),
        "wattn": w_attn_proj[l].astype(BF16),
        "wout": w_out[l].astype(BF16),
    }


def kernel(x_prompt, x_sample, cache_k, cache_v, state_s5_re, state_s5_im, c_prompt, c_sample, w_ada, b_ada, norm_ffn1, w_ffn1_in, w_ffn1_out, norm_mix, w_in, s5_log_dt, s5_a_re, s5_a_im, s5_b_re, s5_b_im, s5_c_re, s5_c_im, s5_d, w_glu, b_glu, w_s5_proj, q_norm, k_norm, attn_sinks, w_attn_proj, w_out, norm_ffn2, w_ffn2_in, w_ffn2_out, norm_final):
    depth = w_ada.shape[0]
    bp, lp, _ = x_prompt.shape
    bs, ls, _ = x_sample.shape
    assert depth == 1 and lp % TILE == 0 and bs * ls == TILE and ls == CHUNK and bs == V7X_SUBLANES
    tiles_per_seq = lp // TILE
    l = 0

    c_all = jnp.concatenate([c_prompt, c_sample], axis=0)
    pad = (-c_all.shape[0]) % (2 * V7X_SUBLANES)
    c_pad = jnp.pad(c_all, ((0, pad), (0, 0)))
    mod = _ada_call(c_pad, w_ada[l], b_ada[l][None]).reshape(c_pad.shape[0], 3 * N_SUBLAYERS, D_MODEL)
    mod_p, mod_s = mod[:bp], mod[bp:bp + bs]

    consts = _mixer_constants(l, w_in, s5_log_dt, s5_a_re, s5_a_im, s5_b_re, s5_b_im, s5_c_re, s5_c_im,
                              s5_d, w_glu, b_glu, w_s5_proj, q_norm, k_norm, attn_sinks, w_attn_proj,
                              w_out, norm_mix)
    w1a, w1b = w_ffn1_in[l].astype(BF16), w_ffn1_out[l].astype(BF16)
    w2a, w2b = w_ffn2_in[l].astype(BF16), w_ffn2_out[l].astype(BF16)
    n1, n2, nf = norm_ffn1[l][None], norm_ffn2[l][None], norm_final[l][None]

    xp = x_prompt.reshape(bp * lp, D_MODEL)
    xs = x_sample.reshape(bs * ls, D_MODEL)
    cache = (cache_k[l].reshape(bs * WINDOW, KV_WIDTH), cache_v[l].reshape(bs * WINDOW, KV_WIDTH),
             _block_layout(state_s5_re[l], state_s5_im[l])[None])

    hp = _ffn_call(xp, mod_p, n1, w1a, w1b, None, nb=1, tiles_per_mod=tiles_per_seq, sub=0, name="ffn1_prompt")
    hs = _ffn_call(xs, mod_s, n1, w1a, w1b, None, nb=bs, tiles_per_mod=1, sub=0, name="ffn1_sample")
    hp, kp, vp, stp = _mixer_call(hp, mod_p, consts, prompt=True, nb=1, tiles_per_seq=tiles_per_seq)
    hs, ks, vs, sts = _mixer_call(hs, mod_s, consts, prompt=False, nb=bs, tiles_per_seq=1, cache=cache)
    yp = _ffn_call(hp, mod_p, n2, w2a, w2b, nf, nb=1, tiles_per_mod=tiles_per_seq, sub=2, name="ffn2_prompt")
    ys = _ffn_call(hs, mod_s, n2, w2a, w2b, nf, nb=bs, tiles_per_mod=1, sub=2, name="ffn2_sample")

    kv_shape = (N_KV_HEADS, HEAD_DIM)
    k_win_p = kp.reshape(bp, lp, KV_WIDTH)[:, lp - WINDOW:].reshape((1, bp, WINDOW) + kv_shape)
    v_win_p = vp.reshape(bp, lp, KV_WIDTH)[:, lp - WINDOW:].reshape((1, bp, WINDOW) + kv_shape)
    s5_re_p, s5_im_p = _from_block_layout(stp[:, 0, :])
    k_win_s = jnp.concatenate([cache_k[l][:, ls:], ks.reshape((bs, ls) + kv_shape)], axis=1)[None]
    v_win_s = jnp.concatenate([cache_v[l][:, ls:], vs.reshape((bs, ls) + kv_shape)], axis=1)[None]
    s5_re_s, s5_im_s = _from_block_layout(sts[0])
    return (yp.reshape(bp, lp, D_MODEL), ys.reshape(bs, ls, D_MODEL), k_win_p, v_win_p,
            s5_re_p[None], s5_im_p[None], k_win_s, v_win_s, s5_re_s[None], s5_im_s[None])
```

```python
import functools
import math

import jax
import jax.numpy as jnp
import numpy as np
from jax import lax
from jax.experimental import pallas as pl
from jax.experimental.pallas import tpu as pltpu

F32 = jnp.float32
BF16 = jnp.bfloat16

D_MODEL = 1024
FF_DIM = 2816
N_SUBLAYERS = 3
FFN_RES = 0.5
RMS_EPS = 1e-6
CHUNK = 64
PAST_LEN = 2048
S5_WIDTH = 512
S5_GROUP = 16
S5_GROUPS = 32
S5_STATE = 64
HEAD_DIM = 64
N_HEADS = 8
N_KV_HEADS = 2
GQA_REP = N_HEADS // N_KV_HEADS
WINDOW = 128
WIN_CHUNKS = WINDOW // CHUNK
ROPE_THETA = 500000.0
ROPE_DIM = HEAD_DIM // 4
ATTN_SCALE = HEAD_DIM ** -0.5
NEG_INF = -1e30
Q_WIDTH = N_HEADS * HEAD_DIM
KV_WIDTH = N_KV_HEADS * HEAD_DIM
QKV_END = S5_WIDTH + Q_WIDTH + 2 * KV_WIDTH
IN_WIDTH = QKV_END + 2 * D_MODEL

V7X_LANES = 128
V7X_SUBLANES = 8
V7X_VMEM_BYTES = 64 * 1024 * 1024

TILE = 512
SEG = TILE // V7X_SUBLANES
FF_COLS = 256
S5_BLOCKS = 4
S5_BLOCK_GROUPS = S5_GROUPS // S5_BLOCKS
S5_BLOCK_STATES = S5_BLOCK_GROUPS * S5_STATE
S5_BLOCK_IN = S5_BLOCK_GROUPS * S5_GROUP
STATE_COLS = S5_BLOCKS * 2 * S5_BLOCK_STATES
KV_SPAN = (WIN_CHUNKS + 1) * CHUNK


def _dot(a, b):
    return jnp.dot(a, b, preferred_element_type=F32)


def _sigmoid(x):
    return 0.5 * jnp.tanh(0.5 * x) + 0.5


def _silu(x):
    half = 0.5 * x
    return half * jnp.tanh(half) + half


def _rms(x):
    return x * lax.rsqrt(jnp.mean(x * x, axis=-1, keepdims=True) + RMS_EPS)


def _per_seq(v, rows):
    nb, d = v.shape
    if nb == 1:
        return v
    return jnp.broadcast_to(v[:, None, :], (nb, rows // nb, d)).reshape(rows, d)


def _modulated(x, norm_w, shift, scale):
    rows = x.shape[0]
    return (_rms(x) * norm_w) * (1.0 + _per_seq(scale, rows)) + _per_seq(shift, rows)


def _ada_kernel(c_ref, w_ref, b_ref, o_ref):
    c = c_ref[...]
    s = _silu(c).astype(BF16)
    o_ref[...] = _dot(s, w_ref[...].astype(BF16)) + b_ref[...]


def _ada_call(c_pad, w_ada, b_ada):
    rows = c_pad.shape[0]
    n = w_ada.shape[1]
    bn = 1024
    return pl.pallas_call(
        _ada_kernel,
        grid=(n // bn,),
        in_specs=[pl.BlockSpec((rows, D_MODEL), lambda i: (0, 0)),
                  pl.BlockSpec((D_MODEL, bn), lambda i: (0, i)),
                  pl.BlockSpec((1, bn), lambda i: (0, i))],
        out_specs=pl.BlockSpec((rows, bn), lambda i: (0, i)),
        out_shape=jax.ShapeDtypeStruct((rows, n), F32),
        compiler_params=pltpu.CompilerParams(dimension_semantics=("arbitrary",)),
        name="adaln",
    )(c_pad, w_ada, b_ada)


def _s5_prep_kernel(ldt_ref, are_ref, aim_ref, bre_ref, bim_ref, bbre_ref, bbim_ref, pre_ref, pim_ref):
    dt = jnp.exp(ldt_ref[...])
    lr = jnp.minimum(are_ref[...], -1e-4)
    li = aim_ref[...]
    mag = jnp.exp(lr * dt)
    ab_re = mag * jnp.cos(li * dt)
    ab_im = mag * jnp.sin(li * dt)
    nr, ni = ab_re - 1.0, ab_im
    den = lr * lr + li * li
    f_re = (nr * lr + ni * li) / den
    f_im = (ni * lr - nr * li) / den
    for g in range(S5_GROUPS):
        fr, fi = f_re[g:g + 1, :], f_im[g:g + 1, :]
        br, bi = bre_ref[g], bim_ref[g]
        bbre_ref[g] = fr * br - fi * bi
        bbim_ref[g] = fr * bi + fi * br
    pr, pi = ab_re, ab_im
    for j in range(SEG):
        pre_ref[j] = pr
        pim_ref[j] = pi
        pr, pi = pr * ab_re - pi * ab_im, pr * ab_im + pi * ab_re


def _s5_prep_call(log_dt, a_re, a_im, b_re_t, b_im_t):
    g, p, gc = S5_GROUPS, S5_STATE, S5_GROUP
    return pl.pallas_call(
        _s5_prep_kernel,
        out_shape=(jax.ShapeDtypeStruct((g, gc, p), F32), jax.ShapeDtypeStruct((g, gc, p), F32),
                   jax.ShapeDtypeStruct((SEG, g, p), F32), jax.ShapeDtypeStruct((SEG, g, p), F32)),
        name="s5_discretise",
    )(log_dt, a_re, a_im, b_re_t, b_im_t)


def _ffn_kernel(*refs, sub, final_norm):
    if final_norm:
        x_ref, mod_ref, nw_ref, w1_ref, w2_ref, nf_ref, o_ref, hid_ref = refs
    else:
        x_ref, mod_ref, nw_ref, w1_ref, w2_ref, o_ref, hid_ref = refs
    x = x_ref[...]
    rows = x.shape[0]
    shift, scale, gate = mod_ref[:, 3 * sub, :], mod_ref[:, 3 * sub + 1, :], mod_ref[:, 3 * sub + 2, :]
    u = _modulated(x, nw_ref[...], shift, scale).astype(BF16)
    for c in range(FF_DIM // FF_COLS):
        lo = c * FF_COLS
        a = _dot(u, w1_ref[:, lo:lo + FF_COLS])
        b = _dot(u, w1_ref[:, FF_DIM + lo:FF_DIM + lo + FF_COLS])
        hid_ref[:, lo:lo + FF_COLS] = (_silu(a) * b).astype(BF16)
    y = _dot(hid_ref[...], w2_ref[...])
    out = x + _per_seq(FFN_RES * gate, rows) * y
    if final_norm:
        out = _rms(out) * nf_ref[...]
    o_ref[...] = out


def _resident(shape):
    zeros = (0,) * len(shape)
    return pl.BlockSpec(shape, lambda i: zeros, pipeline_mode=pl.Buffered(1))


def _ffn_call(x2d, mod, norm_w, w1, w2, norm_final, *, nb, tiles_per_mod, sub, name):
    n = x2d.shape[0]
    mod_all, mod_row = mod
    final_norm = norm_final is not None
    in_specs = [pl.BlockSpec((TILE, D_MODEL), lambda i: (i, 0)),
                pl.BlockSpec((nb, 3 * N_SUBLAYERS, D_MODEL), lambda i: (mod_row // nb + i // tiles_per_mod, 0, 0)),
                _resident((1, D_MODEL)),
                _resident((D_MODEL, 2 * FF_DIM)),
                _resident((FF_DIM, D_MODEL))]
    args = [x2d, mod_all, norm_w, w1, w2]
    if final_norm:
        in_specs.append(_resident((1, D_MODEL)))
        args.append(norm_final)
    return pl.pallas_call(
        functools.partial(_ffn_kernel, sub=sub, final_norm=final_norm),
        grid=(n // TILE,),
        in_specs=in_specs,
        out_specs=pl.BlockSpec((TILE, D_MODEL), lambda i: (i, 0)),
        out_shape=jax.ShapeDtypeStruct((n, D_MODEL), F32),
        scratch_shapes=[pltpu.VMEM((TILE, FF_DIM), BF16)],
        compiler_params=pltpu.CompilerParams(dimension_semantics=("arbitrary",),
                                             vmem_limit_bytes=48 * 1024 * 1024),
        name=name,
    )(*args)


def _mixer_kernel(*refs, prompt, tiles_per_seq):
    (h_ref, mod_ref, nmix_ref, win_ref, qn_ref, kn_ref, ones_ref, invf_ref, m1_ref, m2_ref, esel_ref,
     sinks_ref, bbd_ref, cbd_ref, apow_ref, dvec_ref, wglu_ref, bglu_ref, ws5_ref, wattn_ref,
     wout_ref) = refs[:21]
    if prompt:
        (h2_ref, kout_ref, vout_ref, stout_ref,
         us_ref, perm_ref, bu_ref, yperm_ref, qs_ref, kt_ref, vt_ref, os_ref,
         carry_ref, kcar_ref, vcar_ref, rope_ref) = refs[21:]
    else:
        (ck_ref, cv_ref, stin_ref,
         h2_ref, kout_ref, vout_ref, stout_ref,
         us_ref, perm_ref, bu_ref, yperm_ref, qs_ref, kt_ref, vt_ref, os_ref,
         ktc_ref, vtc_ref) = refs[21:]

    i = pl.program_id(0)
    tile_in_seq = i % tiles_per_seq
    rows = h_ref.shape[0]
    n_chunks = rows // CHUNK
    h = h_ref[...]
    shift, scale, gate = mod_ref[:, 3, :], mod_ref[:, 4, :], mod_ref[:, 5, :]
    u = _modulated(h, nmix_ref[...], shift, scale).astype(BF16)

    s5_u = _dot(u, win_ref[:, 0:S5_WIDTH])
    for b in range(S5_BLOCKS):
        us_ref[b] = s5_u[:, b * S5_BLOCK_IN:(b + 1) * S5_BLOCK_IN]
    q_raw = _dot(u, win_ref[:, S5_WIDTH:S5_WIDTH + Q_WIDTH])
    kv_raw = _dot(u, win_ref[:, S5_WIDTH + Q_WIDTH:QKV_END])
    k_raw, v_new = kv_raw[:, :KV_WIDTH], kv_raw[:, KV_WIDTH:]

    def head_rms(x, w_row, ones):
        ms = _dot((x * x).astype(BF16), ones) * (1.0 / HEAD_DIM)
        return x * lax.rsqrt(ms + RMS_EPS) * w_row

    row_id = lax.broadcasted_iota(jnp.int32, (rows, V7X_LANES), 0)
    if prompt:
        @pl.when(i == 0)
        def _():
            ang_r = row_id.astype(F32) * invf_ref[...]
            rope_ref[0] = jnp.cos(ang_r)
            rope_ref[1] = jnp.sin(ang_r)
        start = jnp.broadcast_to((tile_in_seq * rows).astype(F32), (V7X_SUBLANES, V7X_LANES))
        ang_0 = start * invf_ref[...]
        c_0, s_0 = jnp.cos(ang_0)[0:1, :], jnp.sin(ang_0)[0:1, :]
        cs = c_0 * rope_ref[0] - s_0 * rope_ref[1]
        sn = s_0 * rope_ref[0] + c_0 * rope_ref[1]
    else:
        ang = (PAST_LEN + (row_id & (CHUNK - 1))).astype(F32) * invf_ref[...]
        cs, sn = jnp.cos(ang), jnp.sin(ang)
    s_up, s_dn = sn * m1_ref[...], sn * m2_ref[...]

    def rope(xt):
        return (xt * cs + pltpu.roll(xt, V7X_LANES - ROPE_DIM // 2, 1) * s_up
                + pltpu.roll(xt, ROPE_DIM // 2, 1) * s_dn)

    qn = head_rms(q_raw, qn_ref[...], ones_ref[...])
    kn = head_rms(k_raw, kn_ref[...], ones_ref[0:KV_WIDTH, 0:KV_WIDTH])
    q = jnp.concatenate([rope(qn[:, t * V7X_LANES:(t + 1) * V7X_LANES])
                         for t in range(Q_WIDTH // V7X_LANES)], axis=1)
    k_new = rope(kn)
    qs_ref[...] = (q * ATTN_SCALE).astype(BF16)
    kout_ref[...] = k_new
    vout_ref[...] = v_new

    head_of_lane = lax.broadcasted_iota(jnp.int32, (1, GQA_REP * HEAD_DIM), 1) // HEAD_DIM
    if prompt:
        @pl.when(tile_in_seq == 0)
        def _():
            kcar_ref[...] = jnp.zeros_like(kcar_ref)
            vcar_ref[...] = jnp.zeros_like(vcar_ref)
            carry_ref[...] = jnp.zeros_like(carry_ref)
        k_ext = jnp.concatenate([kcar_ref[...], k_new], axis=0).astype(BF16)
        v_ext = jnp.concatenate([vcar_ref[...], v_new], axis=0).astype(BF16)
        kcar_ref[...] = k_new[rows - WINDOW:, :]
        vcar_ref[...] = v_new[rows - WINDOW:, :]
        key_chunk = lax.broadcasted_iota(jnp.int32, (1, KV_SPAN), 1) // CHUNK
    else:
        k_ext, v_ext = k_new.astype(BF16), v_new.astype(BF16)
        k_cache, v_cache = ck_ref[...].astype(BF16), cv_ref[...].astype(BF16)

    mix_cols = D_MODEL // S5_BLOCKS
    gates = [None] * (2 * S5_BLOCKS)
    y_attn = [None] * S5_BLOCKS

    def gate_chunk(k):
        lo = QKV_END + k * mix_cols
        gates[k] = _sigmoid(_dot(u, win_ref[:, lo:lo + mix_cols]))

    def attn_chunk(k):
        y_attn[k] = _dot(os_ref[...], wattn_ref[:, k * mix_cols:(k + 1) * mix_cols])

    rep = GQA_REP * HEAD_DIM
    q_rows = GQA_REP * CHUNK
    for j in range(N_KV_HEADS):
        sel = esel_ref[j]
        n_ext = k_ext.shape[0]
        kt_ref[0:n_ext, :] = _dot(k_ext, sel).astype(BF16)
        vt_ref[0:n_ext, :] = _dot(v_ext, sel).astype(BF16)
        if not prompt:
            ktc_ref[...] = _dot(k_cache, sel).astype(BF16)
            vtc_ref[...] = _dot(v_cache, sel).astype(BF16)
        sink_b = jnp.concatenate(
            [jnp.full((CHUNK, V7X_LANES), sinks_ref[GQA_REP * j + r], F32) for r in range(GQA_REP)], axis=0)
        lo = j * rep

        for c in range(n_chunks):
            r0 = c * CHUNK
            qc = qs_ref[pl.ds(r0, CHUNK), lo:lo + rep]
            qm = jnp.concatenate([jnp.where(head_of_lane == r, qc, jnp.zeros_like(qc))
                                  for r in range(GQA_REP)], axis=0)
            if prompt:
                keys = kt_ref[pl.ds(r0, KV_SPAN), :]
                vals = vt_ref[pl.ds(r0, KV_SPAN), :]
            else:
                c0 = c * WINDOW
                keys = jnp.concatenate([ktc_ref[pl.ds(c0, WINDOW), :], kt_ref[pl.ds(r0, CHUNK), :]], axis=0)
                vals = jnp.concatenate([vtc_ref[pl.ds(c0, WINDOW), :], vt_ref[pl.ds(r0, CHUNK), :]], axis=0)
            s = lax.dot_general(qm, keys, (((1,), (1,)), ((), ())), preferred_element_type=F32)
            if prompt and c < WIN_CHUNKS:
                first_chunk = tile_in_seq * n_chunks + c - WIN_CHUNKS
                s = jnp.where(first_chunk + key_chunk >= 0, s, NEG_INF)
            m_b = jnp.maximum(jnp.broadcast_to(jnp.max(s, axis=1, keepdims=True), (q_rows, V7X_LANES)), sink_b)
            e = jnp.concatenate([jnp.exp(s[:, :V7X_LANES] - m_b),
                                 jnp.exp(s[:, V7X_LANES:] - m_b[:, :KV_SPAN - V7X_LANES])], axis=1)
            den = jnp.broadcast_to(jnp.sum(e, axis=1, keepdims=True), (q_rows, V7X_LANES)) + jnp.exp(sink_b - m_b)
            inv = 1.0 / den
            pv = _dot(e.astype(BF16), vals) * jnp.concatenate([inv] * (rep // V7X_LANES), axis=1)
            oc = jnp.zeros((CHUNK, rep), F32)
            for r in range(GQA_REP):
                oc = oc + jnp.where(head_of_lane == r, pv[r * CHUNK:(r + 1) * CHUNK, :], 0.0)
            os_ref[pl.ds(r0, CHUNK), lo:lo + rep] = oc.astype(BF16)
            if c % 4 == 3:
                gate_chunk(j * (n_chunks // 4) + c // 4)

    ns = S5_BLOCK_STATES
    groups_per_seg = SEG // V7X_SUBLANES
    part_tiles = ns // V7X_LANES

    def pow_bcast(t0, j):
        r0 = V7X_SUBLANES * j
        return jnp.concatenate([apow_ref[t0 + k, r0:r0 + V7X_SUBLANES, :] for k in range(part_tiles)], axis=1)

    def pow_row(t0, j):
        r0 = V7X_SUBLANES * j
        return jnp.concatenate([apow_ref[t0 + k, r0:r0 + 1, :] for k in range(part_tiles)], axis=1)

    for b in range(S5_BLOCKS):
        c_re, c_im = 2 * ns * b, 2 * ns * b + ns
        for j in range(SEG):
            perm_ref[b, V7X_SUBLANES * j:V7X_SUBLANES * (j + 1), :] = (
                us_ref.at[b][pl.ds(j, V7X_SUBLANES, stride=SEG), :])
        bu_ref[...] = _dot(perm_ref[b].astype(BF16), bbd_ref[b]).reshape(SEG, V7X_SUBLANES, 2 * ns)
        gate_chunk(S5_BLOCKS + b)
        t_re, t_im = c_re // V7X_LANES, c_im // V7X_LANES
        a_re, a_im = pow_bcast(t_re, 0), pow_bcast(t_im, 0)

        def scan_step(j, hc, a_re=a_re, a_im=a_im):
            hr, hi = hc
            nr = a_re * hr - a_im * hi + bu_ref[j, :, 0:ns]
            ni = a_re * hi + a_im * hr + bu_ref[j, :, ns:2 * ns]
            bu_ref[j, :, 0:ns] = nr
            bu_ref[j, :, ns:2 * ns] = ni
            return nr, ni

        zero = jnp.zeros((V7X_SUBLANES, ns), F32)
        end_re, end_im = lax.fori_loop(0, SEG, scan_step, (zero, zero), unroll=True)

        if prompt:
            p_re, p_im = pow_row(t_re, SEG - 1), pow_row(t_im, SEG - 1)
            s_re, s_im = carry_ref[0:1, c_re:c_re + ns], carry_ref[0:1, c_im:c_im + ns]
            in_re, in_im = [], []
            for s in range(V7X_SUBLANES):
                in_re.append(s_re)
                in_im.append(s_im)
                s_re, s_im = (p_re * s_re - p_im * s_im + end_re[s:s + 1, :],
                              p_re * s_im + p_im * s_re + end_im[s:s + 1, :])
            cin_re, cin_im = jnp.concatenate(in_re, axis=0), jnp.concatenate(in_im, axis=0)
            carry_ref[0:1, c_re:c_re + ns] = s_re
            carry_ref[0:1, c_im:c_im + ns] = s_im
            stout_ref[0, :, c_re:c_re + ns] = s_re
            stout_ref[0, :, c_im:c_im + ns] = s_im
        else:
            cin_re, cin_im = stin_ref[0, :, c_re:c_re + ns], stin_ref[0, :, c_im:c_im + ns]

        for j in range(SEG):
            pr, pi = pow_bcast(t_re, j), pow_bcast(t_im, j)
            bu_ref[j, :, 0:ns] = bu_ref[j, :, 0:ns] + (pr * cin_re - pi * cin_im)
            bu_ref[j, :, ns:2 * ns] = bu_ref[j, :, ns:2 * ns] + (pr * cin_im + pi * cin_re)
        attn_chunk(b)
        if not prompt:
            stout_ref[0, :, c_re:c_re + ns] = bu_ref[SEG - 1, :, 0:ns]
            stout_ref[0, :, c_im:c_im + ns] = bu_ref[SEG - 1, :, ns:2 * ns]
        hb = bu_ref[...].reshape(rows, 2 * ns).astype(BF16)
        yperm_ref[b] = _dot(hb, cbd_ref[b])
        d_row = dvec_ref[:, b * S5_BLOCK_IN:(b + 1) * S5_BLOCK_IN]
        for g in range(rows // V7X_SUBLANES):
            s, j0 = g // groups_per_seg, V7X_SUBLANES * (g % groups_per_seg)
            r0 = V7X_SUBLANES * g
            perm_ref[b, r0:r0 + V7X_SUBLANES, :] = (
                yperm_ref.at[b][pl.ds(V7X_SUBLANES * j0 + s, V7X_SUBLANES, stride=V7X_SUBLANES), :]
                + d_row * us_ref[b, r0:r0 + V7X_SUBLANES, :])
    y = jax.nn.gelu(jnp.concatenate([perm_ref[b] for b in range(S5_BLOCKS)], axis=1))
    y = y * _sigmoid(_dot(y.astype(BF16), wglu_ref[...]) + bglu_ref[...])
    y_s5 = _dot(y.astype(BF16), ws5_ref[...])

    merged = jnp.concatenate(
        [gates[k] * y_s5[:, k * mix_cols:(k + 1) * mix_cols] + gates[S5_BLOCKS + k] * y_attn[k]
         for k in range(S5_BLOCKS)], axis=1).astype(BF16)
    h2_ref[...] = h + _per_seq(gate, rows) * _dot(merged, wout_ref[...])


def _mixer_call(h2d, mod, consts, *, prompt, nb, tiles_per_seq, cache=None):
    n = h2d.shape[0]
    mod_all, mod_row = mod
    n_seq_blocks = n // (TILE * tiles_per_seq)
    state_rows = 1 if prompt else V7X_SUBLANES
    smem = pl.BlockSpec(memory_space=pltpu.SMEM)
    in_specs = [pl.BlockSpec((TILE, D_MODEL), lambda i: (i, 0)),
                pl.BlockSpec((nb, 3 * N_SUBLAYERS, D_MODEL), lambda i: (mod_row // nb + i // tiles_per_seq, 0, 0))]
    args = [h2d, mod_all]
    for name in ("nmix", "win", "qn", "kn", "ones", "invf", "m1", "m2", "esel"):
        in_specs.append(_resident(consts[name].shape))
        args.append(consts[name])
    in_specs.append(smem)
    args.append(consts["sinks"])
    for name in ("bbd", "cbd", "apow", "dvec", "wglu", "bglu", "ws5", "wattn", "wout"):
        in_specs.append(_resident(consts[name].shape))
        args.append(consts[name])
    if not prompt:
        ck, cv, st_in = cache
        in_specs += [_resident(ck.shape), _resident(cv.shape), _resident(st_in.shape)]
        args += [ck, cv, st_in]
    out_specs = (pl.BlockSpec((TILE, D_MODEL), lambda i: (i, 0)),
                 pl.BlockSpec((TILE, KV_WIDTH), lambda i: (i, 0)),
                 pl.BlockSpec((TILE, KV_WIDTH), lambda i: (i, 0)),
                 pl.BlockSpec((1, state_rows, STATE_COLS), lambda i: (i // tiles_per_seq, 0, 0)))
    out_shape = (jax.ShapeDtypeStruct((n, D_MODEL), F32),
                 jax.ShapeDtypeStruct((n, KV_WIDTH), F32),
                 jax.ShapeDtypeStruct((n, KV_WIDTH), F32),
                 jax.ShapeDtypeStruct((n_seq_blocks, state_rows, STATE_COLS), F32))
    s5_tiles = (S5_BLOCKS, TILE, S5_BLOCK_IN)
    scratch = [pltpu.VMEM(s5_tiles, F32),
               pltpu.VMEM(s5_tiles, F32),
               pltpu.VMEM((SEG, V7X_SUBLANES, 2 * S5_BLOCK_STATES), F32),
               pltpu.VMEM(s5_tiles, F32),
               pltpu.VMEM((TILE, Q_WIDTH), BF16),
               pltpu.VMEM((TILE + WINDOW, GQA_REP * HEAD_DIM), BF16),
               pltpu.VMEM((TILE + WINDOW, GQA_REP * HEAD_DIM), BF16),
               pltpu.VMEM((TILE, Q_WIDTH), BF16)]
    if prompt:
        scratch += [pltpu.VMEM((1, STATE_COLS), F32),
                    pltpu.VMEM((WINDOW, KV_WIDTH), F32),
                    pltpu.VMEM((WINDOW, KV_WIDTH), F32),
                    pltpu.VMEM((2, TILE, V7X_LANES), F32)]
    else:
        scratch += [pltpu.VMEM((V7X_SUBLANES * WINDOW, GQA_REP * HEAD_DIM), BF16),
                    pltpu.VMEM((V7X_SUBLANES * WINDOW, GQA_REP * HEAD_DIM), BF16)]
    return pl.pallas_call(
        functools.partial(_mixer_kernel, prompt=prompt, tiles_per_seq=tiles_per_seq),
        grid=(n // TILE,),
        in_specs=in_specs,
        out_specs=out_specs,
        out_shape=out_shape,
        scratch_shapes=scratch,
        compiler_params=pltpu.CompilerParams(dimension_semantics=("arbitrary",),
                                             vmem_limit_bytes=60 * 1024 * 1024),
        name="mixer_prompt" if prompt else "mixer_sample",
    )(*args)


def _block_layout(re, im):
    lead = re.shape[:-2]
    re = re.reshape(lead + (S5_BLOCKS, S5_BLOCK_STATES))
    im = im.reshape(lead + (S5_BLOCKS, S5_BLOCK_STATES))
    return jnp.stack([re, im], axis=-2).reshape(lead + (STATE_COLS,))


def _from_block_layout(x):
    lead = x.shape[:-1]
    x = x.reshape(lead + (S5_BLOCKS, 2, S5_BLOCK_STATES))
    shape = lead + (S5_GROUPS, S5_STATE)
    return x[..., 0, :].reshape(shape), x[..., 1, :].reshape(shape)


def _mixer_constants(w_in, s5_log_dt, s5_a_re, s5_a_im, s5_b_re, s5_b_im, s5_c_re, s5_c_im, s5_d,
                     w_glu, b_glu, w_s5_proj, q_norm, k_norm, attn_sinks, w_attn_proj, w_out, norm_mix):
    bb_re, bb_im, pow_re, pow_im = _s5_prep_call(
        s5_log_dt[:, None], s5_a_re, s5_a_im,
        jnp.swapaxes(s5_b_re, 1, 2), jnp.swapaxes(s5_b_im, 1, 2))
    eye = np.eye(S5_BLOCK_GROUPS, dtype=np.float32)

    def in_blocks(x):
        x = x.reshape(S5_BLOCKS, S5_BLOCK_GROUPS, S5_GROUP, S5_STATE)
        return jnp.einsum("bgcp,gh->bgchp", x, eye).reshape(S5_BLOCKS, S5_BLOCK_IN, S5_BLOCK_STATES)

    def out_blocks(x):
        x = x.reshape(S5_BLOCKS, S5_BLOCK_GROUPS, S5_GROUP, S5_STATE)
        return jnp.einsum("bgcp,gh->bgphc", x, eye).reshape(S5_BLOCKS, S5_BLOCK_STATES, S5_BLOCK_IN)

    d = np.arange(V7X_LANES) % HEAD_DIM
    half = ROPE_DIM // 2
    inv_freq = ROPE_THETA ** (-jnp.arange(half, dtype=F32) / half)
    freq_of_lane = np.zeros((half, V7X_LANES), np.float32)
    freq_of_lane[d[d < ROPE_DIM] % half, np.nonzero(d < ROPE_DIM)[0]] = 1.0
    esel = np.zeros((N_KV_HEADS, KV_WIDTH, GQA_REP * HEAD_DIM), np.float32)
    for j in range(N_KV_HEADS):
        esel[j, j * HEAD_DIM:(j + 1) * HEAD_DIM] = np.tile(np.eye(HEAD_DIM, dtype=np.float32), (1, GQA_REP))
    ones_bd = np.kron(np.eye(N_HEADS, dtype=np.float32), np.ones((HEAD_DIM, HEAD_DIM), np.float32))
    return {
        "nmix": norm_mix[None],
        "win": w_in.astype(BF16),
        "qn": jnp.tile(q_norm, N_HEADS)[None],
        "kn": jnp.tile(k_norm, N_KV_HEADS)[None],
        "ones": jnp.asarray(ones_bd, BF16),
        "invf": jnp.sum(inv_freq[:, None] * freq_of_lane, axis=0, keepdims=True),
        "m1": jnp.asarray(np.where(d < half, -1.0, 0.0)[None], F32),
        "m2": jnp.asarray(np.where((d >= half) & (d < ROPE_DIM), 1.0, 0.0)[None], F32),
        "esel": jnp.asarray(esel, BF16),
        "sinks": attn_sinks,
        "bbd": jnp.concatenate([in_blocks(bb_re), in_blocks(bb_im)], axis=-1).astype(BF16),
        "cbd": jnp.concatenate([out_blocks(s5_c_re), -out_blocks(s5_c_im)], axis=1).astype(BF16),
        "apow": jnp.repeat(jnp.swapaxes(_block_layout(pow_re, pow_im).reshape(
            SEG, STATE_COLS // V7X_LANES, V7X_LANES), 0, 1), V7X_SUBLANES, axis=1),
        "dvec": s5_d.reshape(1, S5_WIDTH),
        "wglu": w_glu.astype(BF16),
        "bglu": b_glu[None],
        "ws5": w_s5_proj.astype(BF16),
        "wattn": w_attn_proj.astype(BF16),
        "wout": w_out.astype(BF16),
    }


def kernel(x_prompt, x_sample, cache_k, cache_v, state_s5_re, state_s5_im, c_prompt, c_sample, w_ada, b_ada, norm_ffn1, w_ffn1_in, w_ffn1_out, norm_mix, w_in, s5_log_dt, s5_a_re, s5_a_im, s5_b_re, s5_b_im, s5_c_re, s5_c_im, s5_d, w_glu, b_glu, w_s5_proj, q_norm, k_norm, attn_sinks, w_attn_proj, w_out, norm_ffn2, w_ffn2_in, w_ffn2_out, norm_final):
    depth = w_ada.shape[0]
    bp, lp, _ = x_prompt.shape
    bs, ls, _ = x_sample.shape
    assert depth == 1 and lp % TILE == 0 and bs * ls == TILE and ls == CHUNK and bs == V7X_SUBLANES
    tiles_per_seq = lp // TILE
    (cache_k, cache_v, state_s5_re, state_s5_im, w_ada, b_ada, norm_ffn1, w_ffn1_in, w_ffn1_out, norm_mix, w_in,
     s5_log_dt, s5_a_re, s5_a_im, s5_b_re, s5_b_im, s5_c_re, s5_c_im, s5_d, w_glu, b_glu, w_s5_proj, q_norm,
     k_norm, attn_sinks, w_attn_proj, w_out, norm_ffn2, w_ffn2_in, w_ffn2_out, norm_final) = [
        a.reshape(a.shape[1:]) for a in (
            cache_k, cache_v, state_s5_re, state_s5_im, w_ada, b_ada, norm_ffn1, w_ffn1_in, w_ffn1_out, norm_mix,
            w_in, s5_log_dt, s5_a_re, s5_a_im, s5_b_re, s5_b_im, s5_c_re, s5_c_im, s5_d, w_glu, b_glu,
            w_s5_proj, q_norm, k_norm, attn_sinks, w_attn_proj, w_out, norm_ffn2, w_ffn2_in, w_ffn2_out,
            norm_final)]

    c_all = jnp.concatenate([c_sample, c_prompt], axis=0)
    pad = (-c_all.shape[0]) % (2 * V7X_SUBLANES)
    c_pad = jnp.pad(c_all, ((0, pad), (0, 0)))
    mod = _ada_call(c_pad, w_ada, b_ada[None]).reshape(c_pad.shape[0], 3 * N_SUBLAYERS, D_MODEL)
    mod_p, mod_s = (mod, bs), (mod, 0)

    consts = _mixer_constants(w_in, s5_log_dt, s5_a_re, s5_a_im, s5_b_re, s5_b_im, s5_c_re, s5_c_im,
                              s5_d, w_glu, b_glu, w_s5_proj, q_norm, k_norm, attn_sinks, w_attn_proj,
                              w_out, norm_mix)
    w1a, w1b = w_ffn1_in.astype(BF16), w_ffn1_out.astype(BF16)
    w2a, w2b = w_ffn2_in.astype(BF16), w_ffn2_out.astype(BF16)
    n1, n2, nf = norm_ffn1[None], norm_ffn2[None], norm_final[None]

    xp = x_prompt.reshape(bp * lp, D_MODEL)
    xs = x_sample.reshape(bs * ls, D_MODEL)
    cache = (cache_k.reshape(bs * WINDOW, KV_WIDTH), cache_v.reshape(bs * WINDOW, KV_WIDTH),
             _block_layout(state_s5_re, state_s5_im)[None])

    hp = _ffn_call(xp, mod_p, n1, w1a, w1b, None, nb=1, tiles_per_mod=tiles_per_seq, sub=0, name="ffn1_prompt")
    hs = _ffn_call(xs, mod_s, n1, w1a, w1b, None, nb=bs, tiles_per_mod=1, sub=0, name="ffn1_sample")
    hp, kp, vp, stp = _mixer_call(hp, mod_p, consts, prompt=True, nb=1, tiles_per_seq=tiles_per_seq)
    hs, ks, vs, sts = _mixer_call(hs, mod_s, consts, prompt=False, nb=bs, tiles_per_seq=1, cache=cache)
    yp = _ffn_call(hp, mod_p, n2, w2a, w2b, nf, nb=1, tiles_per_mod=tiles_per_seq, sub=2, name="ffn2_prompt")
    ys = _ffn_call(hs, mod_s, n2, w2a, w2b, nf, nb=bs, tiles_per_mod=1, sub=2, name="ffn2_sample")

    kv_shape = (N_KV_HEADS, HEAD_DIM)
    k_win_p = kp.reshape(bp, lp, KV_WIDTH)[:, lp - WINDOW:].reshape((1, bp, WINDOW) + kv_shape)
    v_win_p = vp.reshape(bp, lp, KV_WIDTH)[:, lp - WINDOW:].reshape((1, bp, WINDOW) + kv_shape)
    s5_re_p, s5_im_p = _from_block_layout(stp[:, 0, :])
    k_win_s = jnp.concatenate([cache_k[:, ls:], ks.reshape((bs, ls) + kv_shape)], axis=1)[None]
    v_win_s = jnp.concatenate([cache_v[:, ls:], vs.reshape((bs, ls) + kv_shape)], axis=1)[None]
    s5_re_s, s5_im_s = _from_block_layout(sts[0])
    return (yp.reshape(bp, lp, D_MODEL), ys.reshape(bs, ls, D_MODEL), k_win_p, v_win_p,
            s5_re_p[None], s5_im_p[None], k_win_s, v_win_s, s5_re_s[None], s5_im_s[None])
```

```python
import functools
import math

import jax
import jax.numpy as jnp
import numpy as np
from jax import lax
from jax.experimental import pallas as pl
from jax.experimental.pallas import tpu as pltpu

F32 = jnp.float32
BF16 = jnp.bfloat16

D_MODEL = 1024
FF_DIM = 2816
N_SUBLAYERS = 3
FFN_RES = 0.5
RMS_EPS = 1e-6
CHUNK = 64
PAST_LEN = 2048
S5_WIDTH = 512
S5_GROUP = 16
S5_GROUPS = 32
S5_STATE = 64
HEAD_DIM = 64
N_HEADS = 8
N_KV_HEADS = 2
GQA_REP = N_HEADS // N_KV_HEADS
WINDOW = 128
WIN_CHUNKS = WINDOW // CHUNK
ROPE_THETA = 500000.0
ROPE_DIM = HEAD_DIM // 4
ATTN_SCALE = HEAD_DIM ** -0.5
NEG_INF = -1e30
Q_WIDTH = N_HEADS * HEAD_DIM
KV_WIDTH = N_KV_HEADS * HEAD_DIM
QKV_END = S5_WIDTH + Q_WIDTH + 2 * KV_WIDTH
IN_WIDTH = QKV_END + 2 * D_MODEL

V7X_LANES = 128
V7X_SUBLANES = 8
V7X_VMEM_BYTES = 64 * 1024 * 1024

TILE = 512
SEG = TILE // V7X_SUBLANES
FF_COLS = 256
S5_BLOCKS = 4
S5_BLOCK_GROUPS = S5_GROUPS // S5_BLOCKS
S5_BLOCK_STATES = S5_BLOCK_GROUPS * S5_STATE
S5_BLOCK_IN = S5_BLOCK_GROUPS * S5_GROUP
STATE_COLS = S5_BLOCKS * 2 * S5_BLOCK_STATES
KV_SPAN = (WIN_CHUNKS + 1) * CHUNK


def _dot(a, b):
    return jnp.dot(a, b, preferred_element_type=F32)


def _sigmoid(x):
    return 0.5 * jnp.tanh(0.5 * x) + 0.5


def _silu(x):
    half = 0.5 * x
    return half * jnp.tanh(half) + half


def _rms(x):
    return x * lax.rsqrt(jnp.mean(x * x, axis=-1, keepdims=True) + RMS_EPS)


def _per_seq(v, rows):
    nb, d = v.shape
    if nb == 1:
        return v
    return jnp.broadcast_to(v[:, None, :], (nb, rows // nb, d)).reshape(rows, d)


def _modulated(x, norm_w, shift, scale):
    rows = x.shape[0]
    return (_rms(x) * norm_w) * (1.0 + _per_seq(scale, rows)) + _per_seq(shift, rows)


def _ada_kernel(c_ref, w_ref, b_ref, o_ref):
    c = c_ref[...]
    s = _silu(c).astype(BF16)
    o_ref[...] = _dot(s, w_ref[...].astype(BF16)) + b_ref[...]


def _ada_call(c_pad, w_ada, b_ada):
    rows = c_pad.shape[0]
    n = w_ada.shape[1]
    bn = 1024
    return pl.pallas_call(
        _ada_kernel,
        grid=(n // bn,),
        in_specs=[pl.BlockSpec((rows, D_MODEL), lambda i: (0, 0)),
                  pl.BlockSpec((D_MODEL, bn), lambda i: (0, i)),
                  pl.BlockSpec((1, bn), lambda i: (0, i))],
        out_specs=pl.BlockSpec((rows, bn), lambda i: (0, i)),
        out_shape=jax.ShapeDtypeStruct((rows, n), F32),
        compiler_params=pltpu.CompilerParams(dimension_semantics=("arbitrary",)),
        name="adaln",
    )(c_pad, w_ada, b_ada)


def _s5_prep_kernel(ldt_ref, are_ref, aim_ref, bre_ref, bim_ref, bbre_ref, bbim_ref, pre_ref, pim_ref):
    dt = jnp.exp(ldt_ref[...])
    lr = jnp.minimum(are_ref[...], -1e-4)
    li = aim_ref[...]
    mag = jnp.exp(lr * dt)
    ab_re = mag * jnp.cos(li * dt)
    ab_im = mag * jnp.sin(li * dt)
    nr, ni = ab_re - 1.0, ab_im
    den = lr * lr + li * li
    f_re = (nr * lr + ni * li) / den
    f_im = (ni * lr - nr * li) / den
    for g in range(S5_GROUPS):
        fr, fi = f_re[g:g + 1, :], f_im[g:g + 1, :]
        br, bi = bre_ref[g], bim_ref[g]
        bbre_ref[g] = fr * br - fi * bi
        bbim_ref[g] = fr * bi + fi * br
    pr, pi = ab_re, ab_im
    for j in range(SEG):
        pre_ref[j] = pr
        pim_ref[j] = pi
        pr, pi = pr * ab_re - pi * ab_im, pr * ab_im + pi * ab_re


def _s5_prep_call(log_dt, a_re, a_im, b_re_t, b_im_t):
    g, p, gc = S5_GROUPS, S5_STATE, S5_GROUP
    return pl.pallas_call(
        _s5_prep_kernel,
        out_shape=(jax.ShapeDtypeStruct((g, gc, p), F32), jax.ShapeDtypeStruct((g, gc, p), F32),
                   jax.ShapeDtypeStruct((SEG, g, p), F32), jax.ShapeDtypeStruct((SEG, g, p), F32)),
        name="s5_discretise",
    )(log_dt, a_re, a_im, b_re_t, b_im_t)


def _ffn_kernel(*refs, sub, final_norm, n_prompt_tiles, n_cast):
    it = iter(refs)
    xp_ref, xs_ref, modp_ref, mods_ref, nw_ref, w1_ref, w2_ref = (next(it) for _ in range(7))
    nf_ref = next(it) if final_norm else None
    cast_in = [next(it) for _ in range(n_cast)]
    op_ref, os_ref = next(it), next(it)
    cast_out = [next(it) for _ in range(n_cast)]
    hid_ref = next(it)
    i = pl.program_id(0)

    def tile(x_ref, mod_ref, o_ref):
        x = x_ref[...]
        rows = x.shape[0]
        shift, scale, gate = mod_ref[:, 3 * sub, :], mod_ref[:, 3 * sub + 1, :], mod_ref[:, 3 * sub + 2, :]
        u = _modulated(x, nw_ref[...], shift, scale).astype(BF16)
        for c in range(FF_DIM // FF_COLS):
            lo = c * FF_COLS
            a = _dot(u, w1_ref[:, lo:lo + FF_COLS])
            b = _dot(u, w1_ref[:, FF_DIM + lo:FF_DIM + lo + FF_COLS])
            hid_ref[:, lo:lo + FF_COLS] = (_silu(a) * b).astype(BF16)
        y = _dot(hid_ref[...], w2_ref[...])
        out = x + _per_seq(FFN_RES * gate, rows) * y
        if final_norm:
            out = _rms(out) * nf_ref[...]
        o_ref[...] = out

    @pl.when(i < n_prompt_tiles)
    def _():
        tile(xp_ref, modp_ref, op_ref)

    @pl.when(i == n_prompt_tiles)
    def _():
        tile(xs_ref, mods_ref, os_ref)

    for src, dst in zip(cast_in, cast_out):
        dst[...] = src[...].astype(BF16)


def _resident(shape):
    zeros = (0,) * len(shape)
    return pl.BlockSpec(shape, lambda i: zeros, pipeline_mode=pl.Buffered(1))


BF16_SUBLANES = 2 * V7X_SUBLANES


def _cast_steps(rows, max_steps):
    steps = max_steps
    while rows % (steps * BF16_SUBLANES):
        steps //= 2
    return steps


def _ffn_call(xp, xs, mod_all, norm_w, w1, w2, norm_final, to_cast, *, n_sample_seqs, tiles_per_seq, sub, name):
    n_p = xp.shape[0] // TILE
    assert xs.shape[0] == TILE
    final_norm = norm_final is not None
    mod_block = 3 * N_SUBLAYERS
    in_specs = [pl.BlockSpec((TILE, D_MODEL), lambda i: (jnp.minimum(i, n_p - 1), 0)),
                pl.BlockSpec((TILE, D_MODEL), lambda i: (0, 0)),
                pl.BlockSpec((1, mod_block, D_MODEL),
                             lambda i: (n_sample_seqs + jnp.minimum(i, n_p - 1) // tiles_per_seq, 0, 0)),
                pl.BlockSpec((n_sample_seqs, mod_block, D_MODEL), lambda i: (0, 0, 0)),
                _resident((1, D_MODEL)),
                _resident((D_MODEL, 2 * FF_DIM)),
                _resident((FF_DIM, D_MODEL))]
    args = [xp, xs, mod_all, mod_all, norm_w, w1, w2]
    if final_norm:
        in_specs.append(_resident((1, D_MODEL)))
        args.append(norm_final)
    out_specs = [pl.BlockSpec((TILE, D_MODEL), lambda i: (jnp.minimum(i, n_p - 1), 0)),
                 pl.BlockSpec((TILE, D_MODEL), lambda i: (0, 0))]
    out_shape = [jax.ShapeDtypeStruct(xp.shape, F32), jax.ShapeDtypeStruct(xs.shape, F32)]
    for w in to_cast:
        rows, cols = w.shape
        steps = _cast_steps(rows, n_p)
        spec = pl.BlockSpec((rows // steps, cols), lambda i, steps=steps: (jnp.minimum(i, steps - 1), 0))
        in_specs.append(spec)
        args.append(w)
        out_specs.append(spec)
        out_shape.append(jax.ShapeDtypeStruct(w.shape, BF16))
    return pl.pallas_call(
        functools.partial(_ffn_kernel, sub=sub, final_norm=final_norm, n_prompt_tiles=n_p, n_cast=len(to_cast)),
        grid=(n_p + 1,),
        in_specs=in_specs,
        out_specs=out_specs,
        out_shape=out_shape,
        scratch_shapes=[pltpu.VMEM((TILE, FF_DIM), BF16)],
        compiler_params=pltpu.CompilerParams(dimension_semantics=("arbitrary",),
                                             vmem_limit_bytes=48 * 1024 * 1024),
        name=name,
    )(*args)


def _mixer_kernel(*refs, prompt, tiles_per_seq):
    (h_ref, mod_ref, nmix_ref, win_ref, qn_ref, kn_ref, ones_ref, invf_ref, m1_ref, m2_ref, esel_ref,
     sinks_ref, bbd_ref, cbd_ref, apow_ref, dvec_ref, wglu_ref, bglu_ref, ws5_ref, wattn_ref,
     wout_ref) = refs[:21]
    if prompt:
        (h2_ref, kout_ref, vout_ref, stout_ref,
         us_ref, perm_ref, bu_ref, yperm_ref, qs_ref, kt_ref, vt_ref, os_ref,
         carry_ref, kcar_ref, vcar_ref, rope_ref) = refs[21:]
    else:
        (ck_ref, cv_ref, stin_ref,
         h2_ref, kout_ref, vout_ref, stout_ref,
         us_ref, perm_ref, bu_ref, yperm_ref, qs_ref, kt_ref, vt_ref, os_ref,
         ktc_ref, vtc_ref) = refs[21:]

    i = pl.program_id(0)
    tile_in_seq = i % tiles_per_seq
    rows = h_ref.shape[0]
    n_chunks = rows // CHUNK
    h = h_ref[...]
    shift, scale, gate = mod_ref[:, 3, :], mod_ref[:, 4, :], mod_ref[:, 5, :]
    u = _modulated(h, nmix_ref[...], shift, scale).astype(BF16)

    s5_u = _dot(u, win_ref[:, 0:S5_WIDTH])
    for b in range(S5_BLOCKS):
        us_ref[b] = s5_u[:, b * S5_BLOCK_IN:(b + 1) * S5_BLOCK_IN]
    q_raw = _dot(u, win_ref[:, S5_WIDTH:S5_WIDTH + Q_WIDTH])
    kv_raw = _dot(u, win_ref[:, S5_WIDTH + Q_WIDTH:QKV_END])
    k_raw, v_new = kv_raw[:, :KV_WIDTH], kv_raw[:, KV_WIDTH:]

    def head_rms(x, w_row, ones):
        ms = _dot((x * x).astype(BF16), ones) * (1.0 / HEAD_DIM)
        return x * lax.rsqrt(ms + RMS_EPS) * w_row

    row_id = lax.broadcasted_iota(jnp.int32, (rows, V7X_LANES), 0)
    if prompt:
        @pl.when(i == 0)
        def _():
            ang_r = row_id.astype(F32) * invf_ref[...]
            rope_ref[0] = jnp.cos(ang_r)
            rope_ref[1] = jnp.sin(ang_r)
        start = jnp.broadcast_to((tile_in_seq * rows).astype(F32), (V7X_SUBLANES, V7X_LANES))
        ang_0 = start * invf_ref[...]
        c_0, s_0 = jnp.cos(ang_0)[0:1, :], jnp.sin(ang_0)[0:1, :]
        cs = c_0 * rope_ref[0] - s_0 * rope_ref[1]
        sn = s_0 * rope_ref[0] + c_0 * rope_ref[1]
    else:
        ang = (PAST_LEN + (row_id & (CHUNK - 1))).astype(F32) * invf_ref[...]
        cs, sn = jnp.cos(ang), jnp.sin(ang)
    s_up, s_dn = sn * m1_ref[...], sn * m2_ref[...]

    def rope(xt):
        return (xt * cs + pltpu.roll(xt, V7X_LANES - ROPE_DIM // 2, 1) * s_up
                + pltpu.roll(xt, ROPE_DIM // 2, 1) * s_dn)

    qn = head_rms(q_raw, qn_ref[...], ones_ref[...])
    kn = head_rms(k_raw, kn_ref[...], ones_ref[0:KV_WIDTH, 0:KV_WIDTH])
    q = jnp.concatenate([rope(qn[:, t * V7X_LANES:(t + 1) * V7X_LANES])
                         for t in range(Q_WIDTH // V7X_LANES)], axis=1)
    k_new = rope(kn)
    qs_ref[...] = (q * ATTN_SCALE).astype(BF16)
    kout_ref[...] = k_new
    vout_ref[...] = v_new

    head_of_lane = lax.broadcasted_iota(jnp.int32, (1, GQA_REP * HEAD_DIM), 1) // HEAD_DIM
    if prompt:
        @pl.when(tile_in_seq == 0)
        def _():
            kcar_ref[...] = jnp.zeros_like(kcar_ref)
            vcar_ref[...] = jnp.zeros_like(vcar_ref)
            carry_ref[...] = jnp.zeros_like(carry_ref)
        k_ext = jnp.concatenate([kcar_ref[...], k_new], axis=0).astype(BF16)
        v_ext = jnp.concatenate([vcar_ref[...], v_new], axis=0).astype(BF16)
        kcar_ref[...] = k_new[rows - WINDOW:, :]
        vcar_ref[...] = v_new[rows - WINDOW:, :]
        key_chunk = lax.broadcasted_iota(jnp.int32, (1, KV_SPAN), 1) // CHUNK
    else:
        k_ext, v_ext = k_new.astype(BF16), v_new.astype(BF16)
        k_cache, v_cache = ck_ref[...].astype(BF16), cv_ref[...].astype(BF16)

    mix_cols = D_MODEL // S5_BLOCKS
    gates = [None] * (2 * S5_BLOCKS)
    y_attn = [None] * S5_BLOCKS

    def gate_chunk(k):
        lo = QKV_END + k * mix_cols
        gates[k] = _sigmoid(_dot(u, win_ref[:, lo:lo + mix_cols]))

    def attn_chunk(k):
        y_attn[k] = _dot(os_ref[...], wattn_ref[:, k * mix_cols:(k + 1) * mix_cols])

    rep = GQA_REP * HEAD_DIM
    q_rows = GQA_REP * CHUNK
    n_ext = k_ext.shape[0]
    for j in range(N_KV_HEADS):
        sel = esel_ref[j]
        kt_ref[j, 0:n_ext, :] = _dot(k_ext, sel).astype(BF16)
        vt_ref[j, 0:n_ext, :] = _dot(v_ext, sel).astype(BF16)
        if not prompt:
            ktc_ref[j] = _dot(k_cache, sel).astype(BF16)
            vtc_ref[j] = _dot(v_cache, sel).astype(BF16)
    sink_cols = [jnp.concatenate([jnp.full((CHUNK, 1), sinks_ref[GQA_REP * j + r], F32)
                                  for r in range(GQA_REP)], axis=0) for j in range(N_KV_HEADS)]

    def attention_chain(j, c):
        r0, lo = c * CHUNK, j * rep
        qc = qs_ref[pl.ds(r0, CHUNK), lo:lo + rep]
        qm = jnp.concatenate([jnp.where(head_of_lane == r, qc, jnp.zeros_like(qc))
                              for r in range(GQA_REP)], axis=0)
        if prompt:
            keys = kt_ref[j, pl.ds(r0, KV_SPAN), :]
            vals = vt_ref[j, pl.ds(r0, KV_SPAN), :]
        else:
            c0 = c * WINDOW
            keys = jnp.concatenate([ktc_ref[j, pl.ds(c0, WINDOW), :], kt_ref[j, pl.ds(r0, CHUNK), :]], axis=0)
            vals = jnp.concatenate([vtc_ref[j, pl.ds(c0, WINDOW), :], vt_ref[j, pl.ds(r0, CHUNK), :]], axis=0)
        s = lax.dot_general(qm, keys, (((1,), (1,)), ((), ())), preferred_element_type=F32)
        if prompt and c < WIN_CHUNKS:
            first_chunk = tile_in_seq * n_chunks + c - WIN_CHUNKS
            s = jnp.where(first_chunk + key_chunk >= 0, s, NEG_INF)
        m = jnp.maximum(jnp.max(s, axis=1, keepdims=True), sink_cols[j])
        e = jnp.exp(s - m)
        den = jnp.sum(e, axis=1, keepdims=True) + jnp.exp(sink_cols[j] - m)
        pv = _dot(e.astype(BF16), vals) * (1.0 / den)
        oc = jnp.zeros((CHUNK, rep), F32)
        for r in range(GQA_REP):
            oc = oc + jnp.where(head_of_lane == r, pv[r * CHUNK:(r + 1) * CHUNK, :], 0.0)
        os_ref[pl.ds(r0, CHUNK), lo:lo + rep] = oc.astype(BF16)

    for j in range(N_KV_HEADS):
        for c in range(n_chunks):
            attention_chain(j, c)

    ns = S5_BLOCK_STATES
    groups_per_seg = SEG // V7X_SUBLANES
    part_tiles = ns // V7X_LANES

    def pow_bcast(t0, j):
        r0 = V7X_SUBLANES * j
        return jnp.concatenate([apow_ref[t0 + k, r0:r0 + V7X_SUBLANES, :] for k in range(part_tiles)], axis=1)

    def pow_row(t0, j):
        r0 = V7X_SUBLANES * j
        return jnp.concatenate([apow_ref[t0 + k, r0:r0 + 1, :] for k in range(part_tiles)], axis=1)

    for b in range(S5_BLOCKS):
        c_re, c_im = 2 * ns * b, 2 * ns * b + ns
        for j in range(SEG):
            perm_ref[b, V7X_SUBLANES * j:V7X_SUBLANES * (j + 1), :] = (
                us_ref.at[b][pl.ds(j, V7X_SUBLANES, stride=SEG), :])
        bu_ref[...] = _dot(perm_ref[b].astype(BF16), bbd_ref[b]).reshape(SEG, V7X_SUBLANES, 2 * ns)
        gate_chunk(2 * b)
        t_re, t_im = c_re // V7X_LANES, c_im // V7X_LANES
        a_re, a_im = pow_bcast(t_re, 0), pow_bcast(t_im, 0)

        end_re = end_im = jnp.zeros((V7X_SUBLANES, ns), F32)
        for j in range(SEG):
            end_re, end_im = (a_re * end_re - a_im * end_im + bu_ref[j, :, 0:ns],
                              a_re * end_im + a_im * end_re + bu_ref[j, :, ns:2 * ns])
            bu_ref[j, :, 0:ns] = end_re
            bu_ref[j, :, ns:2 * ns] = end_im

        if prompt:
            p_re, p_im = pow_row(t_re, SEG - 1), pow_row(t_im, SEG - 1)
            s_re, s_im = carry_ref[0:1, c_re:c_re + ns], carry_ref[0:1, c_im:c_im + ns]
            in_re, in_im = [], []
            for s in range(V7X_SUBLANES):
                in_re.append(s_re)
                in_im.append(s_im)
                s_re, s_im = (p_re * s_re - p_im * s_im + end_re[s:s + 1, :],
                              p_re * s_im + p_im * s_re + end_im[s:s + 1, :])
            cin_re, cin_im = jnp.concatenate(in_re, axis=0), jnp.concatenate(in_im, axis=0)
            carry_ref[0:1, c_re:c_re + ns] = s_re
            carry_ref[0:1, c_im:c_im + ns] = s_im
            stout_ref[0, :, c_re:c_re + ns] = s_re
            stout_ref[0, :, c_im:c_im + ns] = s_im
        else:
            cin_re, cin_im = stin_ref[0, :, c_re:c_re + ns], stin_ref[0, :, c_im:c_im + ns]

        gate_chunk(2 * b + 1)
        for j in range(SEG):
            pr, pi = pow_bcast(t_re, j), pow_bcast(t_im, j)
            bu_ref[j, :, 0:ns] = bu_ref[j, :, 0:ns] + (pr * cin_re - pi * cin_im)
            bu_ref[j, :, ns:2 * ns] = bu_ref[j, :, ns:2 * ns] + (pr * cin_im + pi * cin_re)
        attn_chunk(b)
        if not prompt:
            stout_ref[0, :, c_re:c_re + ns] = bu_ref[SEG - 1, :, 0:ns]
            stout_ref[0, :, c_im:c_im + ns] = bu_ref[SEG - 1, :, ns:2 * ns]
        hb = bu_ref[...].reshape(rows, 2 * ns).astype(BF16)
        yperm_ref[b] = _dot(hb, cbd_ref[b])
        d_row = dvec_ref[:, b * S5_BLOCK_IN:(b + 1) * S5_BLOCK_IN]
        for g in range(rows // V7X_SUBLANES):
            s, j0 = g // groups_per_seg, V7X_SUBLANES * (g % groups_per_seg)
            r0 = V7X_SUBLANES * g
            perm_ref[b, r0:r0 + V7X_SUBLANES, :] = (
                yperm_ref.at[b][pl.ds(V7X_SUBLANES * j0 + s, V7X_SUBLANES, stride=V7X_SUBLANES), :]
                + d_row * us_ref[b, r0:r0 + V7X_SUBLANES, :])
    y = jax.nn.gelu(jnp.concatenate([perm_ref[b] for b in range(S5_BLOCKS)], axis=1))
    y = y * _sigmoid(_dot(y.astype(BF16), wglu_ref[...]) + bglu_ref[...])
    y_s5 = _dot(y.astype(BF16), ws5_ref[...])

    merged = jnp.concatenate(
        [gates[k] * y_s5[:, k * mix_cols:(k + 1) * mix_cols] + gates[S5_BLOCKS + k] * y_attn[k]
         for k in range(S5_BLOCKS)], axis=1).astype(BF16)
    h2_ref[...] = h + _per_seq(gate, rows) * _dot(merged, wout_ref[...])


def _mixer_call(h2d, mod, consts, *, prompt, nb, tiles_per_seq, cache=None):
    n = h2d.shape[0]
    mod_all, mod_row = mod
    n_seq_blocks = n // (TILE * tiles_per_seq)
    state_rows = 1 if prompt else V7X_SUBLANES
    smem = pl.BlockSpec(memory_space=pltpu.SMEM)
    in_specs = [pl.BlockSpec((TILE, D_MODEL), lambda i: (i, 0)),
                pl.BlockSpec((nb, 3 * N_SUBLAYERS, D_MODEL), lambda i: (mod_row // nb + i // tiles_per_seq, 0, 0))]
    args = [h2d, mod_all]
    for name in ("nmix", "win", "qn", "kn", "ones", "invf", "m1", "m2", "esel"):
        in_specs.append(_resident(consts[name].shape))
        args.append(consts[name])
    in_specs.append(smem)
    args.append(consts["sinks"])
    for name in ("bbd", "cbd", "apow", "dvec", "wglu", "bglu", "ws5", "wattn", "wout"):
        in_specs.append(_resident(consts[name].shape))
        args.append(consts[name])
    if not prompt:
        ck, cv, st_in = cache
        in_specs += [_resident(ck.shape), _resident(cv.shape), _resident(st_in.shape)]
        args += [ck, cv, st_in]
    out_specs = (pl.BlockSpec((TILE, D_MODEL), lambda i: (i, 0)),
                 pl.BlockSpec((TILE, KV_WIDTH), lambda i: (i, 0)),
                 pl.BlockSpec((TILE, KV_WIDTH), lambda i: (i, 0)),
                 pl.BlockSpec((1, state_rows, STATE_COLS), lambda i: (i // tiles_per_seq, 0, 0)))
    out_shape = (jax.ShapeDtypeStruct((n, D_MODEL), F32),
                 jax.ShapeDtypeStruct((n, KV_WIDTH), F32),
                 jax.ShapeDtypeStruct((n, KV_WIDTH), F32),
                 jax.ShapeDtypeStruct((n_seq_blocks, state_rows, STATE_COLS), F32))
    s5_tiles = (S5_BLOCKS, TILE, S5_BLOCK_IN)
    scratch = [pltpu.VMEM(s5_tiles, F32),
               pltpu.VMEM(s5_tiles, F32),
               pltpu.VMEM((SEG, V7X_SUBLANES, 2 * S5_BLOCK_STATES), F32),
               pltpu.VMEM(s5_tiles, F32),
               pltpu.VMEM((TILE, Q_WIDTH), BF16),
               pltpu.VMEM((N_KV_HEADS, TILE + WINDOW, GQA_REP * HEAD_DIM), BF16),
               pltpu.VMEM((N_KV_HEADS, TILE + WINDOW, GQA_REP * HEAD_DIM), BF16),
               pltpu.VMEM((TILE, Q_WIDTH), BF16)]
    if prompt:
        scratch += [pltpu.VMEM((1, STATE_COLS), F32),
                    pltpu.VMEM((WINDOW, KV_WIDTH), F32),
                    pltpu.VMEM((WINDOW, KV_WIDTH), F32),
                    pltpu.VMEM((2, TILE, V7X_LANES), F32)]
    else:
        scratch += [pltpu.VMEM((N_KV_HEADS, V7X_SUBLANES * WINDOW, GQA_REP * HEAD_DIM), BF16),
                    pltpu.VMEM((N_KV_HEADS, V7X_SUBLANES * WINDOW, GQA_REP * HEAD_DIM), BF16)]
    return pl.pallas_call(
        functools.partial(_mixer_kernel, prompt=prompt, tiles_per_seq=tiles_per_seq),
        grid=(n // TILE,),
        in_specs=in_specs,
        out_specs=out_specs,
        out_shape=out_shape,
        scratch_shapes=scratch,
        compiler_params=pltpu.CompilerParams(dimension_semantics=("arbitrary",),
                                             vmem_limit_bytes=60 * 1024 * 1024),
        name="mixer_prompt" if prompt else "mixer_sample",
    )(*args)


def _block_layout(re, im):
    lead = re.shape[:-2]
    re = re.reshape(lead + (S5_BLOCKS, S5_BLOCK_STATES))
    im = im.reshape(lead + (S5_BLOCKS, S5_BLOCK_STATES))
    return jnp.stack([re, im], axis=-2).reshape(lead + (STATE_COLS,))


def _from_block_layout(x):
    lead = x.shape[:-1]
    x = x.reshape(lead + (S5_BLOCKS, 2, S5_BLOCK_STATES))
    shape = lead + (S5_GROUPS, S5_STATE)
    return x[..., 0, :].reshape(shape), x[..., 1, :].reshape(shape)


def _mixer_constants(w_in, s5_log_dt, s5_a_re, s5_a_im, s5_b_re, s5_b_im, s5_c_re, s5_c_im, s5_d,
                     w_glu, b_glu, w_s5_proj, q_norm, k_norm, attn_sinks, w_attn_proj, w_out, norm_mix):
    bb_re, bb_im, pow_re, pow_im = _s5_prep_call(
        s5_log_dt[:, None], s5_a_re, s5_a_im,
        jnp.swapaxes(s5_b_re, 1, 2), jnp.swapaxes(s5_b_im, 1, 2))
    eye = np.eye(S5_BLOCK_GROUPS, dtype=np.float32)

    def in_blocks(x):
        x = x.reshape(S5_BLOCKS, S5_BLOCK_GROUPS, S5_GROUP, S5_STATE)
        return jnp.einsum("bgcp,gh->bgchp", x, eye).reshape(S5_BLOCKS, S5_BLOCK_IN, S5_BLOCK_STATES)

    def out_blocks(x):
        x = x.reshape(S5_BLOCKS, S5_BLOCK_GROUPS, S5_GROUP, S5_STATE)
        return jnp.einsum("bgcp,gh->bgphc", x, eye).reshape(S5_BLOCKS, S5_BLOCK_STATES, S5_BLOCK_IN)

    d = np.arange(V7X_LANES) % HEAD_DIM
    half = ROPE_DIM // 2
    inv_freq = ROPE_THETA ** (-jnp.arange(half, dtype=F32) / half)
    freq_of_lane = np.zeros((half, V7X_LANES), np.float32)
    freq_of_lane[d[d < ROPE_DIM] % half, np.nonzero(d < ROPE_DIM)[0]] = 1.0
    esel = np.zeros((N_KV_HEADS, KV_WIDTH, GQA_REP * HEAD_DIM), np.float32)
    for j in range(N_KV_HEADS):
        esel[j, j * HEAD_DIM:(j + 1) * HEAD_DIM] = np.tile(np.eye(HEAD_DIM, dtype=np.float32), (1, GQA_REP))
    ones_bd = np.kron(np.eye(N_HEADS, dtype=np.float32), np.ones((HEAD_DIM, HEAD_DIM), np.float32))
    return {
        "nmix": norm_mix[None],
        "win": w_in.astype(BF16),
        "qn": jnp.tile(q_norm, N_HEADS)[None],
        "kn": jnp.tile(k_norm, N_KV_HEADS)[None],
        "ones": jnp.asarray(ones_bd, BF16),
        "invf": jnp.sum(inv_freq[:, None] * freq_of_lane, axis=0, keepdims=True),
        "m1": jnp.asarray(np.where(d < half, -1.0, 0.0)[None], F32),
        "m2": jnp.asarray(np.where((d >= half) & (d < ROPE_DIM), 1.0, 0.0)[None], F32),
        "esel": jnp.asarray(esel, BF16),
        "sinks": attn_sinks,
        "bbd": jnp.concatenate([in_blocks(bb_re), in_blocks(bb_im)], axis=-1).astype(BF16),
        "cbd": jnp.concatenate([out_blocks(s5_c_re), -out_blocks(s5_c_im)], axis=1).astype(BF16),
        "apow": jnp.repeat(jnp.swapaxes(_block_layout(pow_re, pow_im).reshape(
            SEG, STATE_COLS // V7X_LANES, V7X_LANES), 0, 1), V7X_SUBLANES, axis=1),
        "dvec": s5_d.reshape(1, S5_WIDTH),
        "wglu": w_glu.astype(BF16),
        "bglu": b_glu[None],
        "ws5": w_s5_proj.astype(BF16),
        "wattn": w_attn_proj.astype(BF16),
        "wout": w_out.astype(BF16),
    }


def kernel(x_prompt, x_sample, cache_k, cache_v, state_s5_re, state_s5_im, c_prompt, c_sample, w_ada, b_ada, norm_ffn1, w_ffn1_in, w_ffn1_out, norm_mix, w_in, s5_log_dt, s5_a_re, s5_a_im, s5_b_re, s5_b_im, s5_c_re, s5_c_im, s5_d, w_glu, b_glu, w_s5_proj, q_norm, k_norm, attn_sinks, w_attn_proj, w_out, norm_ffn2, w_ffn2_in, w_ffn2_out, norm_final):
    depth = w_ada.shape[0]
    bp, lp, _ = x_prompt.shape
    bs, ls, _ = x_sample.shape
    assert depth == 1 and lp % TILE == 0 and bs * ls == TILE and ls == CHUNK and bs == V7X_SUBLANES
    tiles_per_seq = lp // TILE
    (cache_k, cache_v, state_s5_re, state_s5_im, w_ada, b_ada, norm_ffn1, w_ffn1_in, w_ffn1_out, norm_mix, w_in,
     s5_log_dt, s5_a_re, s5_a_im, s5_b_re, s5_b_im, s5_c_re, s5_c_im, s5_d, w_glu, b_glu, w_s5_proj, q_norm,
     k_norm, attn_sinks, w_attn_proj, w_out, norm_ffn2, w_ffn2_in, w_ffn2_out, norm_final) = [
        a[0] for a in (
            cache_k, cache_v, state_s5_re, state_s5_im, w_ada, b_ada, norm_ffn1, w_ffn1_in, w_ffn1_out, norm_mix,
            w_in, s5_log_dt, s5_a_re, s5_a_im, s5_b_re, s5_b_im, s5_c_re, s5_c_im, s5_d, w_glu, b_glu,
            w_s5_proj, q_norm, k_norm, attn_sinks, w_attn_proj, w_out, norm_ffn2, w_ffn2_in, w_ffn2_out,
            norm_final)]

    c_all = jnp.concatenate([c_sample, c_prompt], axis=0)
    pad = (-c_all.shape[0]) % (2 * V7X_SUBLANES)
    c_pad = jnp.pad(c_all, ((0, pad), (0, 0)))
    mod = _ada_call(c_pad, w_ada, b_ada[None]).reshape(c_pad.shape[0], 3 * N_SUBLAYERS, D_MODEL)
    mod_p, mod_s = (mod, bs), (mod, 0)

    w1a, w1b = w_ffn1_in.astype(BF16), w_ffn1_out.astype(BF16)
    n1, n2, nf = norm_ffn1[None], norm_ffn2[None], norm_final[None]

    xp = x_prompt.reshape(bp * lp, D_MODEL)
    xs = x_sample.reshape(bs * ls, D_MODEL)
    cache = (cache_k.reshape(bs * WINDOW, KV_WIDTH), cache_v.reshape(bs * WINDOW, KV_WIDTH),
             _block_layout(state_s5_re, state_s5_im)[None])

    later_weights = (w_in, w_glu, w_s5_proj, w_attn_proj, w_out, w_ffn2_in, w_ffn2_out)
    hp, hs, w_in, w_glu, w_s5_proj, w_attn_proj, w_out, w2a, w2b = _ffn_call(
        xp, xs, mod, n1, w1a, w1b, None, later_weights, n_sample_seqs=bs, tiles_per_seq=tiles_per_seq,
        sub=0, name="ffn1")
    consts = _mixer_constants(w_in, s5_log_dt, s5_a_re, s5_a_im, s5_b_re, s5_b_im, s5_c_re, s5_c_im,
                              s5_d, w_glu, b_glu, w_s5_proj, q_norm, k_norm, attn_sinks, w_attn_proj,
                              w_out, norm_mix)
    hp, kp, vp, stp = _mixer_call(hp, mod_p, consts, prompt=True, nb=1, tiles_per_seq=tiles_per_seq)
    hs, ks, vs, sts = _mixer_call(hs, mod_s, consts, prompt=False, nb=bs, tiles_per_seq=1, cache=cache)
    yp, ys = _ffn_call(hp, hs, mod, n2, w2a, w2b, nf, (), n_sample_seqs=bs, tiles_per_seq=tiles_per_seq,
                       sub=2, name="ffn2")

    kv_shape = (N_KV_HEADS, HEAD_DIM)
    k_win_p = kp.reshape(bp, lp, KV_WIDTH)[:, lp - WINDOW:].reshape((1, bp, WINDOW) + kv_shape)
    v_win_p = vp.reshape(bp, lp, KV_WIDTH)[:, lp - WINDOW:].reshape((1, bp, WINDOW) + kv_shape)
    s5_re_p, s5_im_p = _from_block_layout(stp[:, 0, :])
    k_win_s = jnp.concatenate([cache_k[:, ls:], ks.reshape((bs, ls) + kv_shape)], axis=1)[None]
    v_win_s = jnp.concatenate([cache_v[:, ls:], vs.reshape((bs, ls) + kv_shape)], axis=1)[None]
    s5_re_s, s5_im_s = _from_block_layout(sts[0])
    return (yp.reshape(bp, lp, D_MODEL), ys.reshape(bs, ls, D_MODEL), k_win_p, v_win_p,
            s5_re_p[None], s5_im_p[None], k_win_s, v_win_s, s5_re_s[None], s5_im_s[None])
```
